```python
import functools
import jax
import jax.numpy as jnp
from jax import lax
import numpy as np

D_MODEL = 1024
BATCH = 2
SEQ = 8192
DEPTH = 4
DEC_BATCH = 128
DEC_SEQ = 8
PAST_LEN = 2048
PAGE_SIZE = 128

HEAD_DIM = 64
N_HEADS = D_MODEL // 128
ATT_WIDTH = N_HEADS * HEAD_DIM
Q_BLOCK = 128
FORGET_BIAS_INIT = 3.0
POOL_WINDOWS = (2, 4, 8, 16)
N_POOL_GROUPS = len(POOL_WINDOWS)
D_POOL = D_MODEL // 2
POOL_GROUP_DIM = D_POOL // N_POOL_GROUPS
POOL_BUF = max(POOL_WINDOWS) - 1
D_FF = ((8 * D_MODEL // 3 + 255) // 256) * 256
D_IN = D_POOL + 3 * ATT_WIDTH + N_HEADS + 2 * D_MODEL
N_MOD = 6
RMS_EPS = 1e-6

kernel_name = "hybrid_pool_fox_adaln_decode_step"


def rmsnorm(x, g):
    xf = x.astype(jnp.float32)
    y = xf * lax.rsqrt(jnp.mean(xf * xf, axis=-1, keepdims=True) + RMS_EPS)
    return (y * g.astype(jnp.float32)).astype(x.dtype)


def ada_mod(c, w_ada, b_ada):
    m = jax.nn.silu(c) @ w_ada + b_ada
    return jnp.split(m[:, None, :], N_MOD, axis=-1)


def pool_mix(u_ext, pos, w_grp, scale):
    b, total, _ = u_ext.shape
    n = total - POOL_BUF
    cs = jnp.cumsum(u_ext.astype(jnp.float32), axis=1)
    cs = jnp.concatenate([jnp.zeros((b, 1, D_POOL), jnp.float32), cs], axis=1)
    end = cs[:, POOL_BUF + 1:]
    means = []
    for gi, w in enumerate(POOL_WINDOWS):
        sl = slice(gi * POOL_GROUP_DIM, (gi + 1) * POOL_GROUP_DIM)
        start = cs[:, POOL_BUF + 1 - w:POOL_BUF + 1 - w + n, sl]
        cnt = jnp.minimum(w, pos + 1).astype(jnp.float32)[None, :, None]
        means.append((end[..., sl] - start) / cnt)
    mean = jnp.concatenate(means, axis=-1)
    p = (mean - u_ext[:, POOL_BUF:].astype(jnp.float32)).reshape(b, n, N_POOL_GROUPS, POOL_GROUP_DIM)
    y = jnp.einsum('bngc,gcd->bngd', p, w_grp.astype(jnp.float32)).reshape(b, n, D_POOL)
    return y * scale.astype(jnp.float32)


def fox_prompt(q, k, v, logf):
    b, s = q.shape[:2]
    nb = s // Q_BLOCK
    f_cum = jnp.cumsum(logf, axis=1).transpose(0, 2, 1)
    q_blocks = q.reshape(b, nb, Q_BLOCK, N_HEADS, HEAD_DIM).transpose(1, 0, 2, 3, 4)
    f_blocks = f_cum.reshape(b, N_HEADS, nb, Q_BLOCK).transpose(2, 0, 1, 3)
    k_pos = jnp.arange(s)
    scale = HEAD_DIM ** -0.5

    def one_block(args):
        i, qb, fb = args
        logits = jnp.einsum('bqhd,bkhd->bhqk', qb, k, preferred_element_type=jnp.float32) * scale
        logits = logits + fb[..., None] - f_cum[:, :, None, :]
        q_pos = i * Q_BLOCK + jnp.arange(Q_BLOCK)
        logits = jnp.where(k_pos[None, :] <= q_pos[:, None], logits, -jnp.inf)
        p = jax.nn.softmax(logits, axis=-1).astype(v.dtype)
        return jnp.einsum('bhqk,bkhd->bqhd', p, v)

    out = lax.map(one_block, (jnp.arange(nb), q_blocks, f_blocks))
    return out.transpose(1, 0, 2, 3, 4).reshape(b, s, ATT_WIDTH)


def fox_sample(q, k, v, logf, cache_k, cache_v, cache_logf, page_table, layer):
    b, n = q.shape[:2]
    past = page_table.shape[1] * PAGE_SIZE
    k_past = cache_k[layer, page_table].reshape(b, past, N_HEADS, HEAD_DIM).astype(k.dtype)
    v_past = cache_v[layer, page_table].reshape(b, past, N_HEADS, HEAD_DIM).astype(v.dtype)
    lf_past = cache_logf[layer, page_table].reshape(b, past, N_HEADS).astype(jnp.float32)
    k_all = jnp.concatenate([k_past, k], axis=1)
    v_all = jnp.concatenate([v_past, v], axis=1)
    f_cum = jnp.cumsum(jnp.concatenate([lf_past, logf], axis=1), axis=1).transpose(0, 2, 1)
    logits = jnp.einsum('bqhd,bkhd->bhqk', q, k_all, preferred_element_type=jnp.float32) * (HEAD_DIM ** -0.5)
    logits = logits + f_cum[:, :, past:, None] - f_cum[:, :, None, :]
    q_pos = past + jnp.arange(n)
    k_pos = jnp.arange(past + n)
    logits = jnp.where(k_pos[None, :] <= q_pos[:, None], logits, -jnp.inf)
    p = jax.nn.softmax(logits, axis=-1).astype(v_all.dtype)
    return jnp.einsum('bhqk,bkhd->bqhd', p, v_all).reshape(b, n, ATT_WIDTH)


def trunk_layer(x, c, pool_prefix, pos, attend, w_ada, b_ada, g_mix, w_in, b_forget, w_pool_grp,
                pool_scale, w_br_pool, w_br_attn, w_out, g_ffn, w_ffn_in, w_ffn_out):
    b, n = x.shape[:2]
    sh1, sc1, gt1, sh2, sc2, gt2 = ada_mod(c, w_ada, b_ada)
    h = rmsnorm(x, g_mix) * (1 + sc1) + sh1
    z = h @ w_in
    cuts = [int(t) for t in np.cumsum([D_POOL, ATT_WIDTH, ATT_WIDTH, ATT_WIDTH, N_HEADS, D_MODEL])]
    u, q, k, v, fl, ga, gb = jnp.split(z, cuts, axis=-1)
    q = q.reshape(b, n, N_HEADS, HEAD_DIM)
    k = k.reshape(b, n, N_HEADS, HEAD_DIM)
    v = v.reshape(b, n, N_HEADS, HEAD_DIM)
    logf = jax.nn.log_sigmoid(fl.astype(jnp.float32) + b_forget.astype(jnp.float32))
    u_ext = jnp.concatenate([pool_prefix.astype(u.dtype), u], axis=1)
    a_br = pool_mix(u_ext, pos, w_pool_grp, pool_scale).astype(x.dtype)
    b_br = attend(q, k, v, logf)
    merged = jax.nn.sigmoid(ga) * (a_br @ w_br_pool) + jax.nn.sigmoid(gb) * (b_br @ w_br_attn)
    x = x + gt1 * (merged @ w_out)
    h2 = rmsnorm(x, g_ffn) * (1 + sc2) + sh2
    gate, up = jnp.split(h2 @ w_ffn_in, 2, axis=-1)
    x = x + gt2 * ((jax.nn.silu(gate) * up) @ w_ffn_out)
    return x, k, v, logf.astype(x.dtype), u_ext[:, -POOL_BUF:]


def setup_inputs(seed: int = 0) -> dict:
    key = jax.random.key(seed)
    ks = jax.random.split(key, 32)
    n_pages = PAST_LEN // PAGE_SIZE
    n_phys = (DEC_BATCH * n_pages * 5) // 4

    def nrm(k, shape, s=1.0):
        return jax.random.normal(k, shape, jnp.float32) * s

    page_table = jax.random.permutation(ks[0], n_phys)[:DEC_BATCH * n_pages].reshape(DEC_BATCH, n_pages).astype(jnp.int32)
    return {
        'x_prompt': nrm(ks[1], (BATCH, SEQ, D_MODEL)),
        'x_sample': nrm(ks[2], (DEC_BATCH, DEC_SEQ, D_MODEL)),
        'cache_k': nrm(ks[3], (DEPTH, n_phys, PAGE_SIZE, N_HEADS, HEAD_DIM)),
        'cache_v': nrm(ks[4], (DEPTH, n_phys, PAGE_SIZE, N_HEADS, HEAD_DIM)),
        'cache_logf': jax.nn.log_sigmoid(FORGET_BIAS_INIT + nrm(ks[5], (DEPTH, n_phys, PAGE_SIZE, N_HEADS))),
        'state_pool': nrm(ks[6], (DEPTH, DEC_BATCH, POOL_BUF, D_POOL)),
        'page_table': page_table,
        'c_prompt': nrm(ks[7], (BATCH, D_MODEL)),
        'c_sample': nrm(ks[8], (DEC_BATCH, D_MODEL)),
        'w_ada': nrm(ks[9], (DEPTH, D_MODEL, N_MOD * D_MODEL), D_MODEL ** -0.5),
        'b_ada': nrm(ks[10], (DEPTH, N_MOD * D_MODEL), 0.01),
        'g_mix': 1.0 + nrm(ks[11], (DEPTH, D_MODEL), 0.05),
        'w_in': nrm(ks[12], (DEPTH, D_MODEL, D_IN), D_MODEL ** -0.5),
        'b_forget': FORGET_BIAS_INIT + nrm(ks[13], (DEPTH, N_HEADS), 0.1),
        'w_pool_grp': nrm(ks[14], (DEPTH, N_POOL_GROUPS, POOL_GROUP_DIM, POOL_GROUP_DIM), POOL_GROUP_DIM ** -0.5),
        'pool_scale': 1.0 + nrm(ks[15], (DEPTH, D_POOL), 0.1),
        'w_br_pool': nrm(ks[16], (DEPTH, D_POOL, D_MODEL), D_POOL ** -0.5),
        'w_br_attn': nrm(ks[17], (DEPTH, ATT_WIDTH, D_MODEL), ATT_WIDTH ** -0.5),
        'w_out': nrm(ks[18], (DEPTH, D_MODEL, D_MODEL), D_MODEL ** -0.5),
        'g_ffn': 1.0 + nrm(ks[19], (DEPTH, D_MODEL), 0.05),
        'w_ffn_in': nrm(ks[20], (DEPTH, D_MODEL, 2 * D_FF), D_MODEL ** -0.5),
        'w_ffn_out': nrm(ks[21], (DEPTH, D_FF, D_MODEL), D_FF ** -0.5),
        'g_final': 1.0 + nrm(ks[22], (D_MODEL,), 0.05),
    }


def reference(x_prompt, x_sample, cache_k, cache_v, cache_logf, state_pool, page_table, c_prompt, c_sample,
              w_ada, b_ada, g_mix, w_in, b_forget, w_pool_grp, pool_scale, w_br_pool, w_br_attn, w_out,
              g_ffn, w_ffn_in, w_ffn_out, g_final):
    past = page_table.shape[1] * PAGE_SIZE
    pos_p = jnp.arange(x_prompt.shape[1])
    pos_s = past + jnp.arange(x_sample.shape[1])
    prefix_p = jnp.zeros((x_prompt.shape[0], POOL_BUF, D_POOL), x_prompt.dtype)
    xp, xs = x_prompt, x_sample
    kp, vp, lp, pp = [], [], [], []
    ksm, vsm, lsm, psm = [], [], [], []
    for l in range(DEPTH):
        xp, k_, v_, lf_, pl_ = trunk_layer(
            xp, c_prompt, prefix_p, pos_p, fox_prompt,
            w_ada[l], b_ada[l], g_mix[l], w_in[l], b_forget[l], w_pool_grp[l], pool_scale[l],
            w_br_pool[l], w_br_attn[l], w_out[l], g_ffn[l], w_ffn_in[l], w_ffn_out[l])
        kp.append(k_); vp.append(v_); lp.append(lf_); pp.append(pl_)
        attend_s = functools.partial(fox_sample, cache_k=cache_k, cache_v=cache_v, cache_logf=cache_logf,
                                     page_table=page_table, layer=l)
        xs, k_, v_, lf_, pl_ = trunk_layer(
            xs, c_sample, state_pool[l], pos_s, attend_s,
            w_ada[l], b_ada[l], g_mix[l], w_in[l], b_forget[l], w_pool_grp[l], pool_scale[l],
            w_br_pool[l], w_br_attn[l], w_out[l], g_ffn[l], w_ffn_in[l], w_ffn_out[l])
        ksm.append(k_); vsm.append(v_); lsm.append(lf_); psm.append(pl_)
    y_prompt = rmsnorm(xp, g_final)
    y_sample = rmsnorm(xs, g_final)
    return (y_prompt, y_sample,
            jnp.stack(kp), jnp.stack(vp), jnp.stack(lp), jnp.stack(pp),
            jnp.stack(ksm), jnp.stack(vsm), jnp.stack(lsm), jnp.stack(psm))
```

```python
import functools

import jax
import jax.numpy as jnp
from jax import lax
from jax.experimental import pallas as pl
from jax.experimental.pallas import tpu as pltpu

F32 = jnp.float32
BF16 = jnp.bfloat16

LANES = 128
HEAD_DIM = 64
PAGE_SIZE = 128
POOL_WINDOWS = (2, 4, 8, 16)
POOL_GROUP_DIM = 128
POOL_BUF = max(POOL_WINDOWS) - 1
POOL_HALO = 16
N_MOD = 6
RMS_EPS = 1e-6
VMEM_LIMIT = 56 * 1024 * 1024

PROMPT_TILE = 512
ATTN_TILE = 512
ADA_TILE = 512


def _cparams(n_axes):
    return pltpu.CompilerParams(dimension_semantics=("arbitrary",) * n_axes,
                                vmem_limit_bytes=VMEM_LIMIT)


def _const_spec(shape, index):
    return pl.BlockSpec(shape, lambda *_: index, pipeline_mode=pl.Buffered(1))


def _norm_mod(x3, g, sc, sh):
    ms = jnp.mean(x3 * x3, axis=-1, keepdims=True)
    y = x3 * lax.rsqrt(ms + RMS_EPS) * g
    return y * (1.0 + sc) + sh


def _split3(x):
    hi = x.astype(BF16)
    r = x - hi.astype(F32)
    mid = r.astype(BF16)
    lo = (r - mid.astype(F32)).astype(BF16)
    return hi, mid, lo


def _dot(a, b):
    return jnp.dot(a, b, preferred_element_type=F32)


def _dot_nt(a, b):
    return lax.dot_general(a, b, (((1,), (1,)), ((), ())), preferred_element_type=F32)


def _ada_kernel(c_ref, w_ref, b_ref, o_ref):
    c = c_ref[...]
    a = (c * jax.nn.sigmoid(c)).astype(BF16)
    o_ref[0] = _dot(a, w_ref[0].astype(BF16)) + b_ref[0]


def _ada_call(c_all, w_ada, b_ada):
    depth, d, n_out = w_ada.shape
    nb = c_all.shape[0]
    return pl.pallas_call(
        _ada_kernel,
        grid=(depth, n_out // ADA_TILE),
        in_specs=[
            pl.BlockSpec((nb, d), lambda l, j: (0, 0)),
            pl.BlockSpec((1, d, ADA_TILE), lambda l, j: (l, 0, j)),
            pl.BlockSpec((1, 1, ADA_TILE), lambda l, j: (l, 0, j)),
        ],
        out_specs=pl.BlockSpec((1, nb, ADA_TILE), lambda l, j: (l, 0, j)),
        out_shape=jax.ShapeDtypeStruct((depth, nb, n_out), F32),
        compiler_params=_cparams(2),
        name="ada_mod",
    )(c_all, w_ada, b_ada.reshape(depth, 1, n_out))


def _inproj_kernel(x_ref, sc_ref, sh_ref, g_ref, wm_ref, wf_ref, bf_ref, *refs,
                   seg, carry, prompt):
    if prompt:
        u_ref, q_ref, k_ref, v_ref, kb_ref, vb_ref, lf_ref, fc_ref, fr_ref, carry_sc = refs
    else:
        u_ref, q_ref, k_ref, v_ref, lf_ref, fc_ref = refs
    x3 = x_ref[...]
    g_rows, r_rows, d = x3.shape
    n = g_rows * r_rows
    h = _norm_mod(x3, g_ref[0], sc_ref[...], sh_ref[...]).reshape(n, d).astype(BF16)
    z = _dot(h, wm_ref[0])
    w = z.shape[1] // 4
    u_ref[...] = z[:, :w].reshape(g_rows, r_rows, w)
    q = z[:, w:2 * w] * (HEAD_DIM ** -0.5)
    q_ref[...] = q.reshape(g_rows, r_rows, w).astype(q_ref.dtype)
    k = z[:, 2 * w:3 * w]
    v = z[:, 3 * w:]
    k_ref[...] = k.reshape(g_rows, r_rows, w)
    v_ref[...] = v.reshape(g_rows, r_rows, w)
    if prompt:
        kb_ref[...] = k.reshape(g_rows, r_rows, w).astype(BF16)
        vb_ref[...] = v.reshape(g_rows, r_rows, w).astype(BF16)

    fl = _dot(h, wf_ref[0]) + bf_ref[0]
    logf = jnp.minimum(fl, 0.0) - jnp.log1p(jnp.exp(-jnp.abs(fl)))
    n_heads = lf_ref.shape[-1]
    lane = lax.broadcasted_iota(jnp.int32, (1, LANES), 1)
    logf = jnp.where(lane < n_heads, logf, 0.0)
    lf_ref[...] = logf[:, :n_heads].reshape(g_rows, r_rows, n_heads)

    row = lax.broadcasted_iota(jnp.int32, (n, n), 0)
    col = lax.broadcasted_iota(jnp.int32, (n, n), 1)
    keep = col <= row
    if seg < n:
        keep = keep & ((row // seg) == (col // seg))
    tri = jnp.where(keep, 1.0, 0.0).astype(BF16)
    hi, mid, lo = _split3(logf)
    f = _dot(tri, hi) + _dot(tri, mid) + _dot(tri, lo)
    if carry:
        @pl.when(pl.program_id(1) == 0)
        def _():
            carry_sc[...] = jnp.zeros_like(carry_sc)
        f = f + carry_sc[...]
        carry_sc[...] = f[n - 1:n, :]
    if prompt:
        fc_ref[...] = f[:, :n_heads].reshape(g_rows, r_rows, n_heads)
        fr_ref[0] = f.T[:n_heads, :]
    else:
        fc_ref[...] = f.reshape(g_rows, r_rows, LANES)


def _inproj_prompt(x, sc, sh, g, w_main, w_fl, b_fl, layer, n_heads):
    b, s, d = x.shape
    tm = PROMPT_TILE
    w = w_main.shape[2] // 4
    tile = lambda width: pl.BlockSpec((1, tm, width), lambda bi, i: (bi, i, 0))
    per_batch = pl.BlockSpec((1, 1, d), lambda bi, i: (bi, 0, 0))
    out_shape = [
        jax.ShapeDtypeStruct((b, s, w), F32),
        jax.ShapeDtypeStruct((b, s, w), BF16),
        jax.ShapeDtypeStruct((b, s, w), F32),
        jax.ShapeDtypeStruct((b, s, w), F32),
        jax.ShapeDtypeStruct((b, s, w), BF16),
        jax.ShapeDtypeStruct((b, s, w), BF16),
        jax.ShapeDtypeStruct((b, s, n_heads), F32),
        jax.ShapeDtypeStruct((b, s, n_heads), F32),
        jax.ShapeDtypeStruct((b, n_heads, s), F32),
    ]
    out_specs = [tile(w), tile(w), tile(w), tile(w), tile(w), tile(w), tile(n_heads), tile(n_heads),
                 pl.BlockSpec((1, n_heads, tm), lambda bi, i: (bi, 0, i))]
    return pl.pallas_call(
        functools.partial(_inproj_kernel, seg=s, carry=True, prompt=True),
        grid=(b, s // tm),
        in_specs=[
            tile(d), per_batch, per_batch,
            _const_spec((1, 1, d), (layer, 0, 0)),
            _const_spec((1,) + w_main.shape[1:], (layer, 0, 0)),
            _const_spec((1,) + w_fl.shape[1:], (layer, 0, 0)),
            _const_spec((1, 1, LANES), (layer, 0, 0)),
        ],
        out_specs=out_specs,
        out_shape=out_shape,
        scratch_shapes=[pltpu.VMEM((1, LANES), F32)],
        compiler_params=_cparams(2),
        name="inproj_prompt",
    )(x, sc, sh, g, w_main, w_fl, b_fl)


def _inproj_sample(x, sc, sh, g, w_main, w_fl, b_fl, layer, n_heads):
    db, n_new, d = x.shape
    w = w_main.shape[2] // 4
    full = lambda width: pl.BlockSpec((db, n_new, width), lambda i: (0, 0, 0))
    per_seq = pl.BlockSpec((db, 1, d), lambda i: (0, 0, 0))
    out_shape = [
        jax.ShapeDtypeStruct((db, n_new, w), F32),
        jax.ShapeDtypeStruct((db, n_new, w), F32),
        jax.ShapeDtypeStruct((db, n_new, w), F32),
        jax.ShapeDtypeStruct((db, n_new, w), F32),
        jax.ShapeDtypeStruct((db, n_new, n_heads), F32),
        jax.ShapeDtypeStruct((db, n_new, LANES), F32),
    ]
    out_specs = [full(w), full(w), full(w), full(w), full(n_heads), full(LANES)]
    return pl.pallas_call(
        functools.partial(_inproj_kernel, seg=n_new, carry=False, prompt=False),
        grid=(1,),
        in_specs=[
            full(d), per_seq, per_seq,
            _const_spec((1, 1, d), (layer, 0, 0)),
            _const_spec((1,) + w_main.shape[1:], (layer, 0, 0)),
            _const_spec((1,) + w_fl.shape[1:], (layer, 0, 0)),
            _const_spec((1, 1, LANES), (layer, 0, 0)),
        ],
        out_specs=out_specs,
        out_shape=out_shape,
        compiler_params=_cparams(1),
        name="inproj_sample",
    )(x, sc, sh, g, w_main, w_fl, b_fl)


def _attn_prompt_kernel(q_ref, k_ref, v_ref, fr_ref, fc_ref, o_ref, m_sc, l_sc, acc_sc):
    t = q_ref.shape[1]
    hp = pl.program_id(1)
    i = pl.program_id(2)
    q2 = q_ref[0]
    fc = fc_ref[0]
    head_lane = lax.broadcasted_iota(jnp.int32, (1, fc.shape[1]), 1)
    lane = lax.broadcasted_iota(jnp.int32, (1, LANES), 1)
    row = lax.broadcasted_iota(jnp.int32, (t, t), 0)
    col = lax.broadcasted_iota(jnp.int32, (t, t), 1)
    outs = []
    for a in range(LANES // HEAD_DIM):
        head = hp * (LANES // HEAD_DIM) + a
        fq = jnp.sum(jnp.where(head_lane == head, fc, 0.0), axis=1, keepdims=True)
        qa = jnp.where((lane // HEAD_DIM) == a, q2, jnp.zeros_like(q2))
        m_sc[...] = jnp.full_like(m_sc, -jnp.inf)
        l_sc[...] = jnp.zeros_like(l_sc)
        acc_sc[...] = jnp.zeros_like(acc_sc)

        def block(j, diagonal):
            off = pl.multiple_of(j * t, t)
            kj = k_ref[0, pl.ds(off, t), :]
            vj = v_ref[0, pl.ds(off, t), :]
            fk = fr_ref[0, 0, a:a + 1, pl.ds(off, t)]
            s = _dot_nt(qa, kj) + (fq - fk)
            if diagonal:
                s = jnp.where(col <= row, s, -jnp.inf)
            m_old = m_sc[...]
            m_new = jnp.maximum(m_old, jnp.max(s, axis=1, keepdims=True))
            alpha = jnp.exp(m_old - m_new)
            p = jnp.exp(s - m_new)
            l_sc[...] = alpha * l_sc[...] + jnp.sum(p, axis=1, keepdims=True)
            acc_sc[...] = alpha * acc_sc[...] + _dot(p.astype(BF16), vj)
            m_sc[...] = m_new

        def body(j, c):
            block(j, False)
            return c

        lax.fori_loop(0, i, body, 0)
        block(i, True)
        outs.append(acc_sc[...] / l_sc[...])
    o_ref[0] = jnp.where(lane < HEAD_DIM, outs[0], outs[1]).astype(o_ref.dtype)


def _attn_prompt(qb, kb, vb, f_row, f_col):
    b, s, w = qb.shape
    n_heads = f_col.shape[-1]
    pairs = w // LANES
    t = ATTN_TILE
    f_row4 = f_row.reshape(b, pairs, n_heads // pairs, s)
    return pl.pallas_call(
        _attn_prompt_kernel,
        grid=(b, pairs, s // t),
        in_specs=[
            pl.BlockSpec((1, t, LANES), lambda bi, hp, i: (bi, i, hp)),
            pl.BlockSpec((1, s, LANES), lambda bi, hp, i: (bi, 0, hp)),
            pl.BlockSpec((1, s, LANES), lambda bi, hp, i: (bi, 0, hp)),
            pl.BlockSpec((1, 1, n_heads // pairs, s), lambda bi, hp, i: (bi, hp, 0, 0)),
            pl.BlockSpec((1, t, n_heads), lambda bi, hp, i: (bi, i, 0)),
        ],
        out_specs=pl.BlockSpec((1, t, LANES), lambda bi, hp, i: (bi, i, hp)),
        out_shape=jax.ShapeDtypeStruct((b, s, w), BF16),
        scratch_shapes=[pltpu.VMEM((t, 1), F32), pltpu.VMEM((t, 1), F32), pltpu.VMEM((t, LANES), F32)],
        compiler_params=_cparams(3),
        name="attn_prompt",
    )(qb, kb, vb, f_row4, f_col)


def _attn_sample_kernel(pt_ref, q_ref, kn_ref, vn_ref, fcn_ref, sel_ref, *refs, n_pages, n_heads):
    del pt_ref
    k_refs = refs[:n_pages]
    v_refs = refs[n_pages:2 * n_pages]
    lf_refs = refs[2 * n_pages:3 * n_pages]
    o_ref, s_sc, kpad_sc, vpad_sc = refs[3 * n_pages:]
    n_new, w = q_ref.shape[1], q_ref.shape[2]
    rows = n_heads * n_new

    q = q_ref[0]
    lane_head = lax.broadcasted_iota(jnp.int32, (n_heads, 1, w), 2) // HEAD_DIM
    head_idx = lax.broadcasted_iota(jnp.int32, (n_heads, 1, w), 0)
    own_lanes = lane_head == head_idx
    q_bd = jnp.where(own_lanes, q[None], 0.0).reshape(rows, w).astype(BF16)

    lf = jnp.concatenate([r[0, 0] for r in lf_refs], axis=0)
    hi, mid, lo = _split3(lf)
    sel = sel_ref[...]
    suffix = _dot(hi, sel) + _dot(mid, sel) + _dot(lo, sel)

    fcn = fcn_ref[0]
    fq = jnp.stack([fcn[:, h:h + 1] for h in range(n_heads)], axis=0)
    fcn_t = jnp.concatenate([fcn, jnp.zeros((LANES - n_new, LANES), F32)], axis=0).T

    kpad_sc[...] = jnp.zeros_like(kpad_sc)
    vpad_sc[...] = jnp.zeros_like(vpad_sc)
    kpad_sc[0:n_new, :] = kn_ref[0].astype(BF16)
    vpad_sc[0:n_new, :] = vn_ref[0].astype(BF16)
    s_new = _dot_nt(q_bd, kpad_sc[...]).reshape(n_heads, n_new, PAGE_SIZE)
    s_new = s_new + (fq - fcn_t[:n_heads, :][:, None, :])
    tq = lax.broadcasted_iota(jnp.int32, (1, n_new, PAGE_SIZE), 1)
    tk = lax.broadcasted_iota(jnp.int32, (1, n_new, PAGE_SIZE), 2)
    s_sc[:, :, n_pages * PAGE_SIZE:] = jnp.where(tk <= tq, s_new, -jnp.inf)

    later = [jnp.zeros((1, 1), F32) for _ in range(n_heads)]
    for p in reversed(range(n_pages)):
        s_p = _dot_nt(q_bd, k_refs[p][0, 0].astype(BF16)).reshape(n_heads, n_new, PAGE_SIZE)
        bias_rows = []
        for h in range(n_heads):
            within = suffix[p:p + 1, h * PAGE_SIZE:(h + 1) * PAGE_SIZE]
            bias_rows.append(within + later[h])
            later[h] = later[h] + within[:, 0:1] + lf[p:p + 1, h:h + 1]
        bias = jnp.concatenate(bias_rows, axis=0)[:, None, :]
        s_sc[:, :, p * PAGE_SIZE:(p + 1) * PAGE_SIZE] = s_p + (fq + bias)

    s_all = s_sc[...]
    m = jnp.max(s_all, axis=2, keepdims=True)
    e = jnp.exp(s_all - m)
    denom = jnp.sum(e, axis=2, keepdims=True)
    e = e.reshape(rows, (n_pages + 1) * PAGE_SIZE).astype(BF16)
    acc = _dot(e[:, n_pages * PAGE_SIZE:], vpad_sc[...])
    for p in range(n_pages):
        acc = acc + _dot(e[:, p * PAGE_SIZE:(p + 1) * PAGE_SIZE], v_refs[p][0, 0].astype(BF16))
    o3 = acc.reshape(n_heads, n_new, w) / denom
    o_ref[0] = jnp.sum(jnp.where(own_lanes, o3, 0.0), axis=0)


def _attn_sample(q, k_new, v_new, f_new, sel, cache_k4, cache_v4, cache_lf4, page_table, layer, n_heads):
    db, n_new, w = q.shape
    n_pages = page_table.shape[1]
    lf_w = cache_lf4.shape[-1]
    per_seq = lambda width: pl.BlockSpec((1, n_new, width), lambda b, pt: (b, 0, 0))

    def page_spec(shape, p):
        return pl.BlockSpec(shape, lambda b, pt: (layer, pt[b, p], 0, 0))

    in_specs = [per_seq(w), per_seq(w), per_seq(w), per_seq(LANES),
                pl.BlockSpec(sel.shape, lambda b, pt: (0, 0), pipeline_mode=pl.Buffered(1))]
    in_specs += [page_spec((1, 1, PAGE_SIZE, w), p) for p in range(n_pages)]
    in_specs += [page_spec((1, 1, PAGE_SIZE, w), p) for p in range(n_pages)]
    in_specs += [page_spec((1, 1, 1, lf_w), p) for p in range(n_pages)]
    grid_spec = pltpu.PrefetchScalarGridSpec(
        num_scalar_prefetch=1,
        grid=(db,),
        in_specs=in_specs,
        out_specs=pl.BlockSpec((1, n_new, w), lambda b, pt: (b, 0, 0)),
        scratch_shapes=[pltpu.VMEM((n_heads, n_new, (n_pages + 1) * PAGE_SIZE), F32),
                        pltpu.VMEM((PAGE_SIZE, w), BF16), pltpu.VMEM((PAGE_SIZE, w), BF16)],
    )
    return pl.pallas_call(
        functools.partial(_attn_sample_kernel, n_pages=n_pages, n_heads=n_heads),
        grid_spec=grid_spec,
        out_shape=jax.ShapeDtypeStruct((db, n_new, w), F32),
        compiler_params=_cparams(1),
        name="attn_sample",
    )(page_table, q, k_new, v_new, f_new, sel,
      *([cache_k4] * n_pages), *([cache_v4] * n_pages), *([cache_lf4] * n_pages))


def _pool_branch(load, pos, wgrp_ref, ps_ref):
    outs = []
    for gi, win in enumerate(POOL_WINDOWS):
        lanes = slice(gi * POOL_GROUP_DIM, (gi + 1) * POOL_GROUP_DIM)
        cur = load(0, lanes)
        acc = cur
        for dist in range(1, win):
            acc = acc + load(dist, lanes)
        cnt = jnp.minimum(float(win), pos + 1.0)
        p = acc / cnt - cur
        outs.append(_dot(p.astype(BF16), wgrp_ref[0, gi]))
    return jnp.concatenate(outs, axis=1) * ps_ref[0]


def _mix_tail(x3, a_br, b_br, sc, sh, gt, g, wgate_ref, wbrp_ref, wbra_ref, wout_ref):
    g_rows, r_rows, d = x3.shape
    n = g_rows * r_rows
    h = _norm_mod(x3, g, sc, sh).reshape(n, d).astype(BF16)
    gates = _dot(h, wgate_ref[0])
    pool_out = _dot(a_br.astype(BF16), wbrp_ref[0])
    attn_out = _dot(b_br.astype(BF16), wbra_ref[0])
    merged = jax.nn.sigmoid(gates[:, :d]) * pool_out + jax.nn.sigmoid(gates[:, d:]) * attn_out
    y = _dot(merged.astype(BF16), wout_ref[0])
    return x3 + gt * y.reshape(g_rows, r_rows, d)


def _mix_prompt_kernel(x_ref, halo_ref, u_ref, bbr_ref, sc_ref, sh_ref, gt_ref, g_ref, wgate_ref, wgrp_ref,
                       ps_ref, wbrp_ref, wbra_ref, wout_ref, o_ref, ext_sc):
    i = pl.program_id(1)
    tm = x_ref.shape[1]

    @pl.when(i == 0)
    def _():
        ext_sc[0:POOL_HALO, :] = jnp.zeros((POOL_HALO, ext_sc.shape[1]), F32)

    @pl.when(i > 0)
    def _():
        ext_sc[0:POOL_HALO, :] = halo_ref[0]

    ext_sc[POOL_HALO:POOL_HALO + tm, :] = u_ref[0]
    pos = (i * tm + lax.broadcasted_iota(jnp.int32, (tm, 1), 0)).astype(F32)
    load = lambda dist, lanes: ext_sc[pl.ds(POOL_HALO - dist, tm), lanes]
    a_br = _pool_branch(load, pos, wgrp_ref, ps_ref)
    o_ref[...] = _mix_tail(x_ref[...], a_br, bbr_ref[0], sc_ref[...], sh_ref[...], gt_ref[...], g_ref[0],
                           wgate_ref, wbrp_ref, wbra_ref, wout_ref)


def _mix_sample_kernel(x_ref, ext_ref, bbr_ref, sc_ref, sh_ref, gt_ref, g_ref, wgate_ref, wgrp_ref,
                       ps_ref, wbrp_ref, wbra_ref, wout_ref, o_ref, *, past):
    db, n_new, _ = x_ref.shape
    n = db * n_new
    pos = (past + lax.broadcasted_iota(jnp.int32, (db, n_new, 1), 1)).astype(F32).reshape(n, 1)
    load = lambda dist, lanes: ext_ref[:, pl.ds(POOL_HALO - dist, n_new), lanes].reshape(n, POOL_GROUP_DIM)
    a_br = _pool_branch(load, pos, wgrp_ref, ps_ref)
    b_br = bbr_ref[...].reshape(n, bbr_ref.shape[2])
    o_ref[...] = _mix_tail(x_ref[...], a_br, b_br, sc_ref[...], sh_ref[...], gt_ref[...], g_ref[0],
                           wgate_ref, wbrp_ref, wbra_ref, wout_ref)


def _mix_weight_specs(layer, d, w_gate, w_grp, pool_scale3, w_brp, w_bra, w_out):
    return [
        _const_spec((1, 1, d), (layer, 0, 0)),
        _const_spec((1,) + w_gate.shape[1:], (layer, 0, 0)),
        _const_spec((1,) + w_grp.shape[1:], (layer, 0, 0, 0)),
        _const_spec((1,) + pool_scale3.shape[1:], (layer, 0, 0)),
        _const_spec((1,) + w_brp.shape[1:], (layer, 0, 0)),
        _const_spec((1,) + w_bra.shape[1:], (layer, 0, 0)),
        _const_spec((1,) + w_out.shape[1:], (layer, 0, 0)),
    ]


def _mix_prompt(x, u, b_br, sc, sh, gt, g, w_gate, w_grp, pool_scale3, w_brp, w_bra, w_out, layer):
    b, s, d = x.shape
    tm = PROMPT_TILE
    wp = u.shape[2]
    tile = lambda width: pl.BlockSpec((1, tm, width), lambda bi, i: (bi, i, 0))
    per_batch = pl.BlockSpec((1, 1, d), lambda bi, i: (bi, 0, 0))
    halo = pl.BlockSpec((1, POOL_HALO, wp),
                        lambda bi, i: (bi, jnp.maximum(i * (tm // POOL_HALO) - 1, 0), 0))
    return pl.pallas_call(
        _mix_prompt_kernel,
        grid=(b, s // tm),
        in_specs=[tile(d), halo, tile(wp), tile(b_br.shape[2]), per_batch, per_batch, per_batch]
        + _mix_weight_specs(layer, d, w_gate, w_grp, pool_scale3, w_brp, w_bra, w_out),
        out_specs=tile(d),
        out_shape=jax.ShapeDtypeStruct(x.shape, F32),
        scratch_shapes=[pltpu.VMEM((POOL_HALO + tm, wp), F32)],
        compiler_params=_cparams(2),
        name="mix_prompt",
    )(x, u, u, b_br, sc, sh, gt, g, w_gate, w_grp, pool_scale3, w_brp, w_bra, w_out)


def _mix_sample(x, ext, b_br, sc, sh, gt, g, w_gate, w_grp, pool_scale3, w_brp, w_bra, w_out, layer, past):
    db, n_new, d = x.shape
    full = lambda a: pl.BlockSpec(a.shape, lambda i: (0,) * a.ndim)
    return pl.pallas_call(
        functools.partial(_mix_sample_kernel, past=past),
        grid=(1,),
        in_specs=[full(x), full(ext), full(b_br), full(sc), full(sh), full(gt)]
        + _mix_weight_specs(layer, d, w_gate, w_grp, pool_scale3, w_brp, w_bra, w_out),
        out_specs=full(x),
        out_shape=jax.ShapeDtypeStruct(x.shape, F32),
        compiler_params=_cparams(1),
        name="mix_sample",
    )(x, ext, b_br, sc, sh, gt, g, w_gate, w_grp, pool_scale3, w_brp, w_bra, w_out)


def _ffn_kernel(x_ref, sc_ref, sh_ref, gt_ref, g_ref, wg_ref, wu_ref, wo_ref, gfin_ref, *out_refs,
                n_chunks, final):
    x3 = x_ref[...]
    g_rows, r_rows, d = x3.shape
    n = g_rows * r_rows
    h = _norm_mod(x3, g_ref[0], sc_ref[...], sh_ref[...]).reshape(n, d).astype(BF16)
    d_ff = wg_ref.shape[2]
    ck = d_ff // n_chunks
    y = jnp.zeros((n, d), F32)
    for c in range(n_chunks):
        gate = _dot(h, wg_ref[0, :, c * ck:(c + 1) * ck])
        up = _dot(h, wu_ref[0, :, c * ck:(c + 1) * ck])
        act = (gate * jax.nn.sigmoid(gate) * up).astype(BF16)
        y = y + _dot(act, wo_ref[0, c * ck:(c + 1) * ck, :])
    x_new = x3 + gt_ref[...] * y.reshape(g_rows, r_rows, d)
    out_refs[0][...] = x_new
    if final:
        ms = jnp.mean(x_new * x_new, axis=-1, keepdims=True)
        out_refs[1][...] = x_new * lax.rsqrt(ms + RMS_EPS) * gfin_ref[...]


def _ffn(x, sc, sh, gt, g, w_g, w_u, w_o, g_final, layer, final, rows_per_step, per_row_group_mod):
    n_groups, r_rows, d = x.shape
    if per_row_group_mod:
        grid = (1, 1)
        xspec = pl.BlockSpec(x.shape, lambda bi, i: (0, 0, 0))
        mspec = pl.BlockSpec(sc.shape, lambda bi, i: (0, 0, 0))
    else:
        grid = (n_groups, r_rows // rows_per_step)
        xspec = pl.BlockSpec((1, rows_per_step, d), lambda bi, i: (bi, i, 0))
        mspec = pl.BlockSpec((1, 1, d), lambda bi, i: (bi, 0, 0))
    n_out = 2 if final else 1
    outs = pl.pallas_call(
        functools.partial(_ffn_kernel, n_chunks=FFN_CHUNKS, final=final),
        grid=grid,
        in_specs=[xspec, mspec, mspec, mspec,
                  _const_spec((1, 1, d), (layer, 0, 0)),
                  _const_spec((1,) + w_g.shape[1:], (layer, 0, 0)),
                  _const_spec((1,) + w_u.shape[1:], (layer, 0, 0)),
                  _const_spec((1,) + w_o.shape[1:], (layer, 0, 0)),
                  _const_spec((1, d), (0, 0))],
        out_specs=[xspec] * n_out,
        out_shape=[jax.ShapeDtypeStruct(x.shape, F32)] * n_out,
        compiler_params=_cparams(2),
        name="ffn",
    )(x, sc, sh, gt, g, w_g, w_u, w_o, g_final)
    return outs


FFN_CHUNKS = 2


def _suffix_selector(n_heads):
    src = jnp.arange(PAGE_SIZE * n_heads)
    dst = jnp.arange(n_heads * PAGE_SIZE)
    same_head = (src % n_heads)[:, None] == (dst // PAGE_SIZE)[None, :]
    later = (src // n_heads)[:, None] > (dst % PAGE_SIZE)[None, :]
    return (same_head & later).astype(BF16)


def kernel(x_prompt, x_sample, cache_k, cache_v, cache_logf, state_pool, page_table, c_prompt, c_sample,
           w_ada, b_ada, g_mix, w_in, b_forget, w_pool_grp, pool_scale, w_br_pool, w_br_attn, w_out,
           g_ffn, w_ffn_in, w_ffn_out, g_final):
    b, s, d = x_prompt.shape
    db, n_new, _ = x_sample.shape
    depth = w_ada.shape[0]
    n_heads = b_forget.shape[1]
    att_w = n_heads * HEAD_DIM
    d_pool = pool_scale.shape[1]
    d_ff = w_ffn_out.shape[1]
    n_phys = cache_k.shape[1]
    past = page_table.shape[1] * PAGE_SIZE
    assert s % PROMPT_TILE == 0 and s % ATTN_TILE == 0 and n_new == 8 and d_pool == 4 * POOL_GROUP_DIM

    n_main = d_pool + 3 * att_w
    w_main = w_in[:, :, :n_main].astype(BF16)
    w_fl = jnp.pad(w_in[:, :, n_main:n_main + n_heads], ((0, 0), (0, 0), (0, LANES - n_heads))).astype(BF16)
    w_gate = w_in[:, :, n_main + n_heads:].astype(BF16)
    b_fl = jnp.pad(b_forget, ((0, 0), (0, LANES - n_heads))).reshape(depth, 1, LANES)
    w_grp = w_pool_grp.astype(BF16)
    w_brp = w_br_pool.astype(BF16)
    w_bra = w_br_attn.astype(BF16)
    w_o = w_out.astype(BF16)
    w_fg = w_ffn_in[:, :, :d_ff].astype(BF16)
    w_fu = w_ffn_in[:, :, d_ff:].astype(BF16)
    w_fo = w_ffn_out.astype(BF16)
    g_mix3 = g_mix.reshape(depth, 1, d)
    g_ffn3 = g_ffn.reshape(depth, 1, d)
    pool_scale3 = pool_scale.reshape(depth, 1, d_pool)
    g_fin2 = g_final.reshape(1, d)
    sel = _suffix_selector(n_heads)
    cache_k4 = cache_k.reshape(depth, n_phys, PAGE_SIZE, att_w)
    cache_v4 = cache_v.reshape(depth, n_phys, PAGE_SIZE, att_w)
    cache_lf4 = cache_logf.reshape(depth, n_phys, 1, PAGE_SIZE * n_heads)

    nb = b + db
    nb_pad = -(-nb // 8) * 8
    c_all = jnp.concatenate([c_prompt, c_sample, jnp.zeros((nb_pad - nb, d), F32)], axis=0)
    mods = _ada_call(c_all, w_ada, b_ada).reshape(depth, nb_pad, N_MOD, 1, d)

    xp, xs = x_prompt, x_sample
    kp, vp, lp, pp, ksm, vsm, lsm, psm = [], [], [], [], [], [], [], []
    yp = ys = None
    for l in range(depth):
        final = l == depth - 1
        sh1, sc1, gt1, sh2, sc2, gt2 = [mods[l, :b, m] for m in range(N_MOD)]
        u, qb, k, v, kb, vb, logf, f_col, f_row = _inproj_prompt(
            xp, sc1, sh1, g_mix3, w_main, w_fl, b_fl, l, n_heads)
        b_br = _attn_prompt(qb, kb, vb, f_row, f_col)
        x1 = _mix_prompt(xp, u, b_br, sc1, sh1, gt1, g_mix3, w_gate, w_grp, pool_scale3, w_brp, w_bra, w_o, l)
        outs = _ffn(x1, sc2, sh2, gt2, g_ffn3, w_fg, w_fu, w_fo, g_fin2, l, final, PROMPT_TILE, False)
        xp = outs[0]
        if final:
            yp = outs[1]
        kp.append(k.reshape(b, s, n_heads, HEAD_DIM))
        vp.append(v.reshape(b, s, n_heads, HEAD_DIM))
        lp.append(logf)
        pp.append(u[:, s - POOL_BUF:])

        sh1, sc1, gt1, sh2, sc2, gt2 = [mods[l, b:nb, m] for m in range(N_MOD)]
        u, q, k, v, logf, f_new = _inproj_sample(xs, sc1, sh1, g_mix3, w_main, w_fl, b_fl, l, n_heads)
        b_br = _attn_sample(q, k, v, f_new, sel, cache_k4, cache_v4, cache_lf4, page_table, l, n_heads)
        ext = jnp.concatenate([jnp.zeros((db, POOL_HALO - POOL_BUF, d_pool), F32), state_pool[l], u], axis=1)
        x1 = _mix_sample(xs, ext, b_br, sc1, sh1, gt1, g_mix3, w_gate, w_grp, pool_scale3, w_brp, w_bra, w_o,
                         l, past)
        outs = _ffn(x1, sc2, sh2, gt2, g_ffn3, w_fg, w_fu, w_fo, g_fin2, l, final, None, True)
        xs = outs[0]
        if final:
            ys = outs[1]
        ksm.append(k.reshape(db, n_new, n_heads, HEAD_DIM))
        vsm.append(v.reshape(db, n_new, n_heads, HEAD_DIM))
        lsm.append(logf)
        psm.append(ext[:, ext.shape[1] - POOL_BUF:])

    return (yp, ys, jnp.stack(kp), jnp.stack(vp), jnp.stack(lp), jnp.stack(pp),
            jnp.stack(ksm), jnp.stack(vsm), jnp.stack(lsm), jnp.stack(psm))
```

```python
import functools

import jax
import jax.numpy as jnp
from jax import lax
from jax.experimental import pallas as pl
from jax.experimental.pallas import tpu as pltpu

F32 = jnp.float32
BF16 = jnp.bfloat16

LANES = 128
HEAD_DIM = 64
PAGE_SIZE = 128
POOL_WINDOWS = (2, 4, 8, 16)
POOL_GROUP_DIM = 128
POOL_BUF = max(POOL_WINDOWS) - 1
POOL_HALO = 16
N_MOD = 6
RMS_EPS = 1e-6
LOG2E = 1.4426950408889634
VMEM_LIMIT = 56 * 1024 * 1024

PROMPT_TILE = 512
ATTN_TILE = 1024
ADA_TILE = 512


def _cparams(n_axes):
    return pltpu.CompilerParams(dimension_semantics=("arbitrary",) * n_axes,
                                vmem_limit_bytes=VMEM_LIMIT)


def _const_spec(shape, index):
    return pl.BlockSpec(shape, lambda *_: index, pipeline_mode=pl.Buffered(1))


def _norm_mod(x3, g, sc, sh):
    ms = jnp.mean(x3 * x3, axis=-1, keepdims=True)
    y = x3 * lax.rsqrt(ms + RMS_EPS) * g
    return y * (1.0 + sc) + sh


def _split3(x):
    hi = x.astype(BF16)
    r = x - hi.astype(F32)
    mid = r.astype(BF16)
    lo = (r - mid.astype(F32)).astype(BF16)
    return hi, mid, lo


def _dot(a, b):
    return jnp.dot(a, b, preferred_element_type=F32)


def _dot_nt(a, b):
    return lax.dot_general(a, b, (((1,), (1,)), ((), ())), preferred_element_type=F32)


def _ada_kernel(c_ref, w_ref, b_ref, o_ref):
    c = c_ref[...]
    a = (c * jax.nn.sigmoid(c)).astype(BF16)
    o_ref[0] = _dot(a, w_ref[0].astype(BF16)) + b_ref[0]


def _ada_call(c_all, w_ada, b_ada):
    depth, d, n_out = w_ada.shape
    nb = c_all.shape[0]
    return pl.pallas_call(
        _ada_kernel,
        grid=(depth, n_out // ADA_TILE),
        in_specs=[
            pl.BlockSpec((nb, d), lambda l, j: (0, 0)),
            pl.BlockSpec((1, d, ADA_TILE), lambda l, j: (l, 0, j)),
            pl.BlockSpec((1, 1, ADA_TILE), lambda l, j: (l, 0, j)),
        ],
        out_specs=pl.BlockSpec((1, nb, ADA_TILE), lambda l, j: (l, 0, j)),
        out_shape=jax.ShapeDtypeStruct((depth, nb, n_out), F32),
        compiler_params=_cparams(2),
        name="ada_mod",
    )(c_all, w_ada, b_ada.reshape(depth, 1, n_out))


def _inproj_kernel(x_ref, sc_ref, sh_ref, g_ref, wm_ref, wf_ref, bf_ref, *refs,
                   seg, carry, prompt):
    if prompt:
        u_ref, k_ref, v_ref, lf_ref, qa_ref, ka_ref, vt_ref, carry_sc = refs
    else:
        u_ref, q_ref, k_ref, v_ref, lf_ref, fc_ref = refs
    x3 = x_ref[...]
    g_rows, r_rows, d = x3.shape
    n = g_rows * r_rows
    h = _norm_mod(x3, g_ref[0], sc_ref[...], sh_ref[...]).reshape(n, d).astype(BF16)
    z = _dot(h, wm_ref[0])
    w = z.shape[1] // 4
    u_ref[...] = z[:, :w].reshape(g_rows, r_rows, w)
    q = z[:, w:2 * w] * (HEAD_DIM ** -0.5 * (LOG2E if prompt else 1.0))
    k = z[:, 2 * w:3 * w]
    v = z[:, 3 * w:]
    k_ref[...] = k.reshape(g_rows, r_rows, w)
    v_ref[...] = v.reshape(g_rows, r_rows, w)
    if prompt:
        vt_ref[0] = v.T.astype(BF16)
    else:
        q_ref[...] = q.reshape(g_rows, r_rows, w)

    fl = _dot(h, wf_ref[0]) + bf_ref[0]
    logf = jnp.minimum(fl, 0.0) - jnp.log1p(jnp.exp(-jnp.abs(fl)))
    n_heads = lf_ref.shape[-1]
    lane = lax.broadcasted_iota(jnp.int32, (1, LANES), 1)
    logf = jnp.where(lane < n_heads, logf, 0.0)
    lf_ref[...] = logf[:, :n_heads].reshape(g_rows, r_rows, n_heads)

    row = lax.broadcasted_iota(jnp.int32, (n, n), 0)
    col = lax.broadcasted_iota(jnp.int32, (n, n), 1)
    keep = col <= row
    if seg < n:
        keep = keep & ((row // seg) == (col // seg))
    tri = jnp.where(keep, 1.0, 0.0).astype(BF16)
    hi, mid, lo = _split3(logf)
    f = _dot(tri, hi) + _dot(tri, mid) + _dot(tri, lo)
    if carry:
        @pl.when(pl.program_id(1) == 0)
        def _():
            carry_sc[...] = jnp.zeros_like(carry_sc)
        f = f + carry_sc[...]
        carry_sc[...] = f[n - 1:n, :]
    if not prompt:
        fc_ref[...] = f.reshape(g_rows, r_rows, LANES)
        return
    f2 = f * LOG2E
    for head in range(n_heads):
        pair = slice((head // 2) * LANES, (head // 2 + 1) * LANES)
        q_h, k_h = q[:, pair], k[:, pair]
        if head % 2:
            q_h = pltpu.roll(q_h, HEAD_DIM, axis=1)
            k_h = pltpu.roll(k_h, HEAD_DIM, axis=1)
        fh = jnp.broadcast_to(f2[:, head:head + 1], (n, LANES))
        hi = fh.astype(BF16).astype(F32)
        mid = (fh - hi).astype(BF16).astype(F32)
        lo = fh - hi - mid
        ones = jnp.ones_like(fh)
        zeros = jnp.zeros_like(fh)
        terms = lambda first, second: jnp.where(
            lane < HEAD_DIM + 3, first, jnp.where(lane < HEAD_DIM + 6, second, zeros))
        f_terms = jnp.where(lane % 3 == (HEAD_DIM + 0) % 3, hi, jnp.where(lane % 3 == (HEAD_DIM + 1) % 3, mid, lo))
        qa_ref[0, head] = jnp.where(lane < HEAD_DIM, q_h, terms(f_terms, ones)).astype(BF16)
        ka_ref[0, head] = jnp.where(lane < HEAD_DIM, k_h, terms(ones, -f_terms)).astype(BF16)


def _inproj_prompt(x, sc, sh, g, w_main, w_fl, b_fl, layer, n_heads):
    b, s, d = x.shape
    tm = PROMPT_TILE
    w = w_main.shape[2] // 4
    tile = lambda width: pl.BlockSpec((1, tm, width), lambda bi, i: (bi, i, 0))
    per_batch = pl.BlockSpec((1, 1, d), lambda bi, i: (bi, 0, 0))
    per_head = pl.BlockSpec((1, n_heads, tm, LANES), lambda bi, i: (bi, 0, i, 0))
    out_shape = [
        jax.ShapeDtypeStruct((b, s, w), F32),
        jax.ShapeDtypeStruct((b, s, w), F32),
        jax.ShapeDtypeStruct((b, s, w), F32),
        jax.ShapeDtypeStruct((b, s, n_heads), F32),
        jax.ShapeDtypeStruct((b, n_heads, s, LANES), BF16),
        jax.ShapeDtypeStruct((b, n_heads, s, LANES), BF16),
        jax.ShapeDtypeStruct((b, w, s), BF16),
    ]
    out_specs = [tile(w), tile(w), tile(w), tile(n_heads), per_head, per_head,
                 pl.BlockSpec((1, w, tm), lambda bi, i: (bi, 0, i))]
    return pl.pallas_call(
        functools.partial(_inproj_kernel, seg=s, carry=True, prompt=True),
        grid=(b, s // tm),
        in_specs=[
            tile(d), per_batch, per_batch,
            _const_spec((1, 1, d), (layer, 0, 0)),
            _const_spec((1,) + w_main.shape[1:], (layer, 0, 0)),
            _const_spec((1,) + w_fl.shape[1:], (layer, 0, 0)),
            _const_spec((1, 1, LANES), (layer, 0, 0)),
        ],
        out_specs=out_specs,
        out_shape=out_shape,
        scratch_shapes=[pltpu.VMEM((1, LANES), F32)],
        compiler_params=_cparams(2),
        name="inproj_prompt",
    )(x, sc, sh, g, w_main, w_fl, b_fl)


def _inproj_sample(x, sc, sh, g, w_main, w_fl, b_fl, layer, n_heads):
    db, n_new, d = x.shape
    w = w_main.shape[2] // 4
    full = lambda width: pl.BlockSpec((db, n_new, width), lambda i: (0, 0, 0))
    per_seq = pl.BlockSpec((db, 1, d), lambda i: (0, 0, 0))
    out_shape = [
        jax.ShapeDtypeStruct((db, n_new, w), F32),
        jax.ShapeDtypeStruct((db, n_new, w), F32),
        jax.ShapeDtypeStruct((db, n_new, w), F32),
        jax.ShapeDtypeStruct((db, n_new, w), F32),
        jax.ShapeDtypeStruct((db, n_new, n_heads), F32),
        jax.ShapeDtypeStruct((db, n_new, LANES), F32),
    ]
    out_specs = [full(w), full(w), full(w), full(w), full(n_heads), full(LANES)]
    return pl.pallas_call(
        functools.partial(_inproj_kernel, seg=n_new, carry=False, prompt=False),
        grid=(1,),
        in_specs=[
            full(d), per_seq, per_seq,
            _const_spec((1, 1, d), (layer, 0, 0)),
            _const_spec((1,) + w_main.shape[1:], (layer, 0, 0)),
            _const_spec((1,) + w_fl.shape[1:], (layer, 0, 0)),
            _const_spec((1, 1, LANES), (layer, 0, 0)),
        ],
        out_specs=out_specs,
        out_shape=out_shape,
        compiler_params=_cparams(1),
        name="inproj_sample",
    )(x, sc, sh, g, w_main, w_fl, b_fl)


def _attn_prompt_kernel(qa_ref, ka_ref, vt_ref, o_ref):
    t = qa_ref.shape[2]
    i = pl.program_id(2)
    heads = range(qa_ref.shape[1])
    qs = [qa_ref[0, a] for a in heads]
    key = lax.broadcasted_iota(jnp.int32, (t, t), 0)
    qry = lax.broadcasted_iota(jnp.int32, (t, t), 1)

    def block(j, carry, diagonal):
        off = pl.multiple_of(j * t, t)
        scores = []
        for a in heads:
            s = _dot_nt(ka_ref[0, a, pl.ds(off, t), :], qs[a])
            if diagonal:
                s = jnp.where(key <= qry, s, -jnp.inf)
            scores.append(s)
        out = []
        for a in heads:
            m, l, acc = carry[a]
            vj = vt_ref[0, a * HEAD_DIM:(a + 1) * HEAD_DIM, pl.ds(off, t)]
            m_new = jnp.maximum(m, jnp.max(scores[a], axis=0, keepdims=True))
            alpha = jnp.exp2(m - m_new)
            p = jnp.exp2(scores[a] - m_new)
            l = alpha * l + jnp.sum(p, axis=0, keepdims=True)
            acc = alpha * acc + _dot(vj, p.astype(BF16))
            out.append((m_new, l, acc))
        return tuple(out)

    init = tuple((jnp.full((1, t), -jnp.inf, F32), jnp.zeros((1, t), F32), jnp.zeros((HEAD_DIM, t), F32))
                 for _ in heads)
    carry = lax.fori_loop(0, i, lambda j, c: block(j, c, False), init)
    carry = block(i, carry, True)
    o = jnp.concatenate([acc / l for _, l, acc in carry], axis=0)
    o_ref[0] = o.T.astype(o_ref.dtype)


def _attn_prompt(qa, ka, vt):
    b, n_heads, s, _ = qa.shape
    per_step = LANES // HEAD_DIM
    t = ATTN_TILE
    return pl.pallas_call(
        _attn_prompt_kernel,
        grid=(b, n_heads // per_step, s // t),
        in_specs=[
            pl.BlockSpec((1, per_step, t, LANES), lambda bi, hp, i: (bi, hp, i, 0)),
            pl.BlockSpec((1, per_step, s, LANES), lambda bi, hp, i: (bi, hp, 0, 0)),
            pl.BlockSpec((1, LANES, s), lambda bi, hp, i: (bi, hp, 0)),
        ],
        out_specs=pl.BlockSpec((1, t, LANES), lambda bi, hp, i: (bi, i, hp)),
        out_shape=jax.ShapeDtypeStruct((b, s, n_heads * HEAD_DIM), BF16),
        compiler_params=_cparams(3),
        name="attn_prompt",
    )(qa, ka, vt)


def _attn_sample_kernel(pt_ref, qh_ref, kn_ref, vn_ref, fq_ref, fkn_ref, sel_ref, *refs, n_pages, n_heads):
    del pt_ref
    k_refs = refs[:n_pages]
    v_refs = refs[n_pages:2 * n_pages]
    lf_refs = refs[2 * n_pages:3 * n_pages]
    o_ref, s_sc = refs[3 * n_pages:]
    rows = qh_ref.shape[1]
    n_new = rows // n_heads
    page_w = PAGE_SIZE * n_heads

    qh = qh_ref[0].astype(BF16)
    fq = fq_ref[0]
    row_head = lax.broadcasted_iota(jnp.int32, (rows, 1), 0) // n_new
    row_tok = lax.broadcasted_iota(jnp.int32, (rows, 1), 0) % n_new
    own = row_head == lax.broadcasted_iota(jnp.int32, (1, page_w), 1) % n_heads

    lf = jnp.concatenate([r[0, 0] for r in lf_refs], axis=0)
    parts = jnp.concatenate(_split3(lf), axis=0)
    sums = _dot(parts, sel_ref[...])
    sums = sums[:n_pages] + sums[n_pages:2 * n_pages] + sums[2 * n_pages:]
    within, totals = sums[:, :page_w], sums[:, page_w:]

    later = jnp.zeros((1, page_w), F32)
    for p in reversed(range(n_pages)):
        k2 = k_refs[p][0, 0].reshape(page_w, HEAD_DIM).astype(BF16)
        s_p = _dot_nt(qh, k2) + (fq + (within[p:p + 1] + later))
        later = later + totals[p:p + 1]
        s_sc[:, p * page_w:(p + 1) * page_w] = jnp.where(own, s_p, -jnp.inf)

    col = lax.broadcasted_iota(jnp.int32, (1, LANES), 1)
    valid = (row_head == col % n_heads) & (col // n_heads <= row_tok)
    s_new = _dot_nt(qh, kn_ref[0].astype(BF16)) + (fq - fkn_ref[0])
    s_sc[:, n_pages * page_w:] = jnp.where(valid, s_new, -jnp.inf)

    s_all = s_sc[...]
    m = jnp.max(s_all, axis=1, keepdims=True)
    e = jnp.exp(s_all - m)
    denom = jnp.sum(e, axis=1, keepdims=True)
    e = e.astype(BF16)
    acc = _dot(e[:, n_pages * page_w:], vn_ref[0].astype(BF16))
    for p in range(n_pages):
        v2 = v_refs[p][0, 0].reshape(page_w, HEAD_DIM).astype(BF16)
        acc = acc + _dot(e[:, p * page_w:(p + 1) * page_w], v2)
    o_ref[0] = acc / denom


def _attn_sample(q, k_new, v_new, f_new, sel, cache_k, cache_v, cache_lf4, page_table, layer):
    db, n_new, w = q.shape
    n_heads = w // HEAD_DIM
    n_pages = page_table.shape[1]
    rows = n_heads * n_new
    page_w = PAGE_SIZE * n_heads
    new_keys = LANES // n_heads

    qh = q.reshape(db, n_new, n_heads, HEAD_DIM).transpose(0, 2, 1, 3).reshape(db, rows, HEAD_DIM)
    pad_keys = lambda a: jnp.pad(a.reshape(db, n_new, n_heads, HEAD_DIM),
                                 ((0, 0), (0, new_keys - n_new), (0, 0), (0, 0))).reshape(db, LANES, HEAD_DIM)
    f8 = f_new[:, :, :n_heads]
    fq = f8.transpose(0, 2, 1).reshape(db, rows, 1)
    fkn = jnp.pad(f8.reshape(db, 1, rows), ((0, 0), (0, 0), (0, LANES - rows)))

    per_seq = lambda r, c: pl.BlockSpec((1, r, c), lambda b, pt: (b, 0, 0))

    def page_spec(shape, p):
        return pl.BlockSpec(shape, lambda b, pt: (layer, pt[b, p]) + (0,) * (len(shape) - 2))

    in_specs = [per_seq(rows, HEAD_DIM), per_seq(LANES, HEAD_DIM), per_seq(LANES, HEAD_DIM),
                per_seq(rows, 1), per_seq(1, LANES),
                pl.BlockSpec(sel.shape, lambda b, pt: (0, 0), pipeline_mode=pl.Buffered(1))]
    in_specs += [page_spec((1, 1, PAGE_SIZE, n_heads, HEAD_DIM), p) for p in range(n_pages)]
    in_specs += [page_spec((1, 1, PAGE_SIZE, n_heads, HEAD_DIM), p) for p in range(n_pages)]
    in_specs += [page_spec((1, 1, 1, page_w), p) for p in range(n_pages)]
    grid_spec = pltpu.PrefetchScalarGridSpec(
        num_scalar_prefetch=1,
        grid=(db,),
        in_specs=in_specs,
        out_specs=per_seq(rows, HEAD_DIM),
        scratch_shapes=[pltpu.VMEM((rows, n_pages * page_w + LANES), F32)],
    )
    out = pl.pallas_call(
        functools.partial(_attn_sample_kernel, n_pages=n_pages, n_heads=n_heads),
        grid_spec=grid_spec,
        out_shape=jax.ShapeDtypeStruct((db, rows, HEAD_DIM), F32),
        compiler_params=_cparams(1),
        name="attn_sample",
    )(page_table, qh, pad_keys(k_new), pad_keys(v_new), fq, fkn, sel,
      *([cache_k] * n_pages), *([cache_v] * n_pages), *([cache_lf4] * n_pages))
    return out.reshape(db, n_heads, n_new, HEAD_DIM).transpose(0, 2, 1, 3).reshape(db, n_new, w)


def _pool_branch(load, pos, wgrp_ref, ps_ref):
    outs = []
    for gi, win in enumerate(POOL_WINDOWS):
        lanes = slice(gi * POOL_GROUP_DIM, (gi + 1) * POOL_GROUP_DIM)
        cur = load(0, lanes)
        acc = cur
        for dist in range(1, win):
            acc = acc + load(dist, lanes)
        cnt = jnp.minimum(float(win), pos + 1.0)
        p = acc / cnt - cur
        outs.append(_dot(p.astype(BF16), wgrp_ref[0, gi]))
    return jnp.concatenate(outs, axis=1) * ps_ref[0]


def _mix_tail(x3, a_br, b_br, sc, sh, gt, g, wgate_ref, wbrp_ref, wbra_ref, wout_ref):
    g_rows, r_rows, d = x3.shape
    n = g_rows * r_rows
    h = _norm_mod(x3, g, sc, sh).reshape(n, d).astype(BF16)
    gates = _dot(h, wgate_ref[0])
    pool_out = _dot(a_br.astype(BF16), wbrp_ref[0])
    attn_out = _dot(b_br.astype(BF16), wbra_ref[0])
    merged = jax.nn.sigmoid(gates[:, :d]) * pool_out + jax.nn.sigmoid(gates[:, d:]) * attn_out
    y = _dot(merged.astype(BF16), wout_ref[0])
    return x3 + gt * y.reshape(g_rows, r_rows, d)


def _mix_prompt_kernel(x_ref, halo_ref, u_ref, bbr_ref, sc_ref, sh_ref, gt_ref, g_ref, wgate_ref, wgrp_ref,
                       ps_ref, wbrp_ref, wbra_ref, wout_ref, o_ref, ext_sc):
    i = pl.program_id(1)
    tm = x_ref.shape[1]

    @pl.when(i == 0)
    def _():
        ext_sc[0:POOL_HALO, :] = jnp.zeros((POOL_HALO, ext_sc.shape[1]), F32)

    @pl.when(i > 0)
    def _():
        ext_sc[0:POOL_HALO, :] = halo_ref[0]

    ext_sc[POOL_HALO:POOL_HALO + tm, :] = u_ref[0]
    pos = (i * tm + lax.broadcasted_iota(jnp.int32, (tm, 1), 0)).astype(F32)
    load = lambda dist, lanes: ext_sc[pl.ds(POOL_HALO - dist, tm), lanes]
    a_br = _pool_branch(load, pos, wgrp_ref, ps_ref)
    o_ref[...] = _mix_tail(x_ref[...], a_br, bbr_ref[0], sc_ref[...], sh_ref[...], gt_ref[...], g_ref[0],
                           wgate_ref, wbrp_ref, wbra_ref, wout_ref)


def _mix_sample_kernel(x_ref, ext_ref, bbr_ref, sc_ref, sh_ref, gt_ref, g_ref, wgate_ref, wgrp_ref,
                       ps_ref, wbrp_ref, wbra_ref, wout_ref, o_ref, *, past):
    db, n_new, _ = x_ref.shape
    n = db * n_new
    pos = (past + lax.broadcasted_iota(jnp.int32, (db, n_new, 1), 1)).astype(F32).reshape(n, 1)
    load = lambda dist, lanes: ext_ref[:, pl.ds(POOL_HALO - dist, n_new), lanes].reshape(n, POOL_GROUP_DIM)
    a_br = _pool_branch(load, pos, wgrp_ref, ps_ref)
    b_br = bbr_ref[...].reshape(n, bbr_ref.shape[2])
    o_ref[...] = _mix_tail(x_ref[...], a_br, b_br, sc_ref[...], sh_ref[...], gt_ref[...], g_ref[0],
                           wgate_ref, wbrp_ref, wbra_ref, wout_ref)


def _mix_weight_specs(layer, d, w_gate, w_grp, pool_scale3, w_brp, w_bra, w_out):
    return [
        _const_spec((1, 1, d), (layer, 0, 0)),
        _const_spec((1,) + w_gate.shape[1:], (layer, 0, 0)),
        _const_spec((1,) + w_grp.shape[1:], (layer, 0, 0, 0)),
        _const_spec((1,) + pool_scale3.shape[1:], (layer, 0, 0)),
        _const_spec((1,) + w_brp.shape[1:], (layer, 0, 0)),
        _const_spec((1,) + w_bra.shape[1:], (layer, 0, 0)),
        _const_spec((1,) + w_out.shape[1:], (layer, 0, 0)),
    ]


def _mix_prompt(x, u, b_br, sc, sh, gt, g, w_gate, w_grp, pool_scale3, w_brp, w_bra, w_out, layer):
    b, s, d = x.shape
    tm = PROMPT_TILE
    wp = u.shape[2]
    tile = lambda width: pl.BlockSpec((1, tm, width), lambda bi, i: (bi, i, 0))
    per_batch = pl.BlockSpec((1, 1, d), lambda bi, i: (bi, 0, 0))
    halo = pl.BlockSpec((1, POOL_HALO, wp),
                        lambda bi, i: (bi, jnp.maximum(i * (tm // POOL_HALO) - 1, 0), 0))
    return pl.pallas_call(
        _mix_prompt_kernel,
        grid=(b, s // tm),
        in_specs=[tile(d), halo, tile(wp), tile(b_br.shape[2]), per_batch, per_batch, per_batch]
        + _mix_weight_specs(layer, d, w_gate, w_grp, pool_scale3, w_brp, w_bra, w_out),
        out_specs=tile(d),
        out_shape=jax.ShapeDtypeStruct(x.shape, F32),
        scratch_shapes=[pltpu.VMEM((POOL_HALO + tm, wp), F32)],
        compiler_params=_cparams(2),
        name="mix_prompt",
    )(x, u, u, b_br, sc, sh, gt, g, w_gate, w_grp, pool_scale3, w_brp, w_bra, w_out)


def _mix_sample(x, ext, b_br, sc, sh, gt, g, w_gate, w_grp, pool_scale3, w_brp, w_bra, w_out, layer, past):
    db, n_new, d = x.shape
    full = lambda a: pl.BlockSpec(a.shape, lambda i: (0,) * a.ndim)
    return pl.pallas_call(
        functools.partial(_mix_sample_kernel, past=past),
        grid=(1,),
        in_specs=[full(x), full(ext), full(b_br), full(sc), full(sh), full(gt)]
        + _mix_weight_specs(layer, d, w_gate, w_grp, pool_scale3, w_brp, w_bra, w_out),
        out_specs=full(x),
        out_shape=jax.ShapeDtypeStruct(x.shape, F32),
        compiler_params=_cparams(1),
        name="mix_sample",
    )(x, ext, b_br, sc, sh, gt, g, w_gate, w_grp, pool_scale3, w_brp, w_bra, w_out)


def _ffn_kernel(x_ref, sc_ref, sh_ref, gt_ref, g_ref, wg_ref, wu_ref, wo_ref, gfin_ref, *out_refs,
                n_chunks, final):
    x3 = x_ref[...]
    g_rows, r_rows, d = x3.shape
    n = g_rows * r_rows
    h = _norm_mod(x3, g_ref[0], sc_ref[...], sh_ref[...]).reshape(n, d).astype(BF16)
    d_ff = wg_ref.shape[2]
    ck = d_ff // n_chunks
    y = jnp.zeros((n, d), F32)
    for c in range(n_chunks):
        gate = _dot(h, wg_ref[0, :, c * ck:(c + 1) * ck])
        up = _dot(h, wu_ref[0, :, c * ck:(c + 1) * ck])
        act = (gate * jax.nn.sigmoid(gate) * up).astype(BF16)
        y = y + _dot(act, wo_ref[0, c * ck:(c + 1) * ck, :])
    x_new = x3 + gt_ref[...] * y.reshape(g_rows, r_rows, d)
    out_refs[0][...] = x_new
    if final:
        ms = jnp.mean(x_new * x_new, axis=-1, keepdims=True)
        out_refs[1][...] = x_new * lax.rsqrt(ms + RMS_EPS) * gfin_ref[...]


def _ffn(x, sc, sh, gt, g, w_g, w_u, w_o, g_final, layer, final, rows_per_step, per_row_group_mod):
    n_groups, r_rows, d = x.shape
    if per_row_group_mod:
        grid = (1, 1)
        xspec = pl.BlockSpec(x.shape, lambda bi, i: (0, 0, 0))
        mspec = pl.BlockSpec(sc.shape, lambda bi, i: (0, 0, 0))
    else:
        grid = (n_groups, r_rows // rows_per_step)
        xspec = pl.BlockSpec((1, rows_per_step, d), lambda bi, i: (bi, i, 0))
        mspec = pl.BlockSpec((1, 1, d), lambda bi, i: (bi, 0, 0))
    n_out = 2 if final else 1
    outs = pl.pallas_call(
        functools.partial(_ffn_kernel, n_chunks=FFN_CHUNKS, final=final),
        grid=grid,
        in_specs=[xspec, mspec, mspec, mspec,
                  _const_spec((1, 1, d), (layer, 0, 0)),
                  _const_spec((1,) + w_g.shape[1:], (layer, 0, 0)),
                  _const_spec((1,) + w_u.shape[1:], (layer, 0, 0)),
                  _const_spec((1,) + w_o.shape[1:], (layer, 0, 0)),
                  _const_spec((1, d), (0, 0))],
        out_specs=[xspec] * n_out,
        out_shape=[jax.ShapeDtypeStruct(x.shape, F32)] * n_out,
        compiler_params=_cparams(2),
        name="ffn",
    )(x, sc, sh, gt, g, w_g, w_u, w_o, g_final)
    return outs


FFN_CHUNKS = 2


def _suffix_selector(n_heads):
    idx = jnp.arange(PAGE_SIZE * n_heads)
    same_head = (idx % n_heads)[:, None] == (idx % n_heads)[None, :]
    later = (idx // n_heads)[:, None] > (idx // n_heads)[None, :]
    return jnp.concatenate([same_head & later, same_head], axis=1).astype(BF16)


def kernel(x_prompt, x_sample, cache_k, cache_v, cache_logf, state_pool, page_table, c_prompt, c_sample,
           w_ada, b_ada, g_mix, w_in, b_forget, w_pool_grp, pool_scale, w_br_pool, w_br_attn, w_out,
           g_ffn, w_ffn_in, w_ffn_out, g_final):
    b, s, d = x_prompt.shape
    db, n_new, _ = x_sample.shape
    depth = w_ada.shape[0]
    n_heads = b_forget.shape[1]
    att_w = n_heads * HEAD_DIM
    d_pool = pool_scale.shape[1]
    d_ff = w_ffn_out.shape[1]
    n_phys = cache_k.shape[1]
    past = page_table.shape[1] * PAGE_SIZE
    assert s % PROMPT_TILE == 0 and s % ATTN_TILE == 0 and n_new == 8 and d_pool == 4 * POOL_GROUP_DIM

    n_main = d_pool + 3 * att_w
    w_main = w_in[:, :, :n_main].astype(BF16)
    w_fl = jnp.pad(w_in[:, :, n_main:n_main + n_heads], ((0, 0), (0, 0), (0, LANES - n_heads))).astype(BF16)
    w_gate = w_in[:, :, n_main + n_heads:].astype(BF16)
    b_fl = jnp.pad(b_forget, ((0, 0), (0, LANES - n_heads))).reshape(depth, 1, LANES)
    w_grp = w_pool_grp.astype(BF16)
    w_brp = w_br_pool.astype(BF16)
    w_bra = w_br_attn.astype(BF16)
    w_o = w_out.astype(BF16)
    w_fg = w_ffn_in[:, :, :d_ff].astype(BF16)
    w_fu = w_ffn_in[:, :, d_ff:].astype(BF16)
    w_fo = w_ffn_out.astype(BF16)
    g_mix3 = g_mix.reshape(depth, 1, d)
    g_ffn3 = g_ffn.reshape(depth, 1, d)
    pool_scale3 = pool_scale.reshape(depth, 1, d_pool)
    g_fin2 = g_final.reshape(1, d)
    sel = _suffix_selector(n_heads)
    cache_lf4 = cache_logf.reshape(depth, n_phys, 1, PAGE_SIZE * n_heads)

    nb = b + db
    nb_pad = -(-nb // 8) * 8
    c_all = jnp.concatenate([c_prompt, c_sample, jnp.zeros((nb_pad - nb, d), F32)], axis=0)
    mods = _ada_call(c_all, w_ada, b_ada).reshape(depth, nb_pad, N_MOD, 1, d)

    xp, xs = x_prompt, x_sample
    kp, vp, lp, pp, ksm, vsm, lsm, psm = [], [], [], [], [], [], [], []
    yp = ys = None
    for l in range(depth):
        final = l == depth - 1
        sh1, sc1, gt1, sh2, sc2, gt2 = [mods[l, :b, m] for m in range(N_MOD)]
        u, k, v, logf, qa, ka, vt = _inproj_prompt(xp, sc1, sh1, g_mix3, w_main, w_fl, b_fl, l, n_heads)
        b_br = _attn_prompt(qa, ka, vt)
        x1 = _mix_prompt(xp, u, b_br, sc1, sh1, gt1, g_mix3, w_gate, w_grp, pool_scale3, w_brp, w_bra, w_o, l)
        outs = _ffn(x1, sc2, sh2, gt2, g_ffn3, w_fg, w_fu, w_fo, g_fin2, l, final, PROMPT_TILE, False)
        xp = outs[0]
        if final:
            yp = outs[1]
        kp.append(k.reshape(b, s, n_heads, HEAD_DIM))
        vp.append(v.reshape(b, s, n_heads, HEAD_DIM))
        lp.append(logf)
        pp.append(u[:, s - POOL_BUF:])

        sh1, sc1, gt1, sh2, sc2, gt2 = [mods[l, b:nb, m] for m in range(N_MOD)]
        u, q, k, v, logf, f_new = _inproj_sample(xs, sc1, sh1, g_mix3, w_main, w_fl, b_fl, l, n_heads)
        b_br = _attn_sample(q, k, v, f_new, sel, cache_k, cache_v, cache_lf4, page_table, l)
        ext = jnp.concatenate([jnp.zeros((db, POOL_HALO - POOL_BUF, d_pool), F32), state_pool[l], u], axis=1)
        x1 = _mix_sample(xs, ext, b_br, sc1, sh1, gt1, g_mix3, w_gate, w_grp, pool_scale3, w_brp, w_bra, w_o,
                         l, past)
        outs = _ffn(x1, sc2, sh2, gt2, g_ffn3, w_fg, w_fu, w_fo, g_fin2, l, final, None, True)
        xs = outs[0]
        if final:
            ys = outs[1]
        ksm.append(k.reshape(db, n_new, n_heads, HEAD_DIM))
        vsm.append(v.reshape(db, n_new, n_heads, HEAD_DIM))
        lsm.append(logf)
        psm.append(ext[:, ext.shape[1] - POOL_BUF:])

    return (yp, ys, jnp.stack(kp), jnp.stack(vp), jnp.stack(lp), jnp.stack(pp),
            jnp.stack(ksm), jnp.stack(vsm), jnp.stack(lsm), jnp.stack(psm))
```

```python
import functools

import jax
import jax.numpy as jnp
from jax import lax
from jax.experimental import pallas as pl
from jax.experimental.pallas import tpu as pltpu

F32 = jnp.float32
BF16 = jnp.bfloat16

LANES = 128
HEAD_DIM = 64
PAGE_SIZE = 128
POOL_WINDOWS = (2, 4, 8, 16)
POOL_GROUP_DIM = 128
POOL_BUF = max(POOL_WINDOWS) - 1
POOL_HALO = 16
N_MOD = 6
RMS_EPS = 1e-6
LOG2E = 1.4426950408889634
VMEM_LIMIT = 56 * 1024 * 1024

PROMPT_TILE = 512
ATTN_TILE = 1024
ADA_TILE = 512


def _cparams(n_axes):
    return pltpu.CompilerParams(dimension_semantics=("arbitrary",) * n_axes,
                                vmem_limit_bytes=VMEM_LIMIT)


def _const_spec(shape, index):
    return pl.BlockSpec(shape, lambda *_: index, pipeline_mode=pl.Buffered(1))


def _norm_mod(x3, g, sc, sh):
    ms = jnp.mean(x3 * x3, axis=-1, keepdims=True)
    y = x3 * lax.rsqrt(ms + RMS_EPS) * g
    return y * (1.0 + sc) + sh


def _split3(x):
    hi = x.astype(BF16)
    r = x - hi.astype(F32)
    mid = r.astype(BF16)
    lo = (r - mid.astype(F32)).astype(BF16)
    return hi, mid, lo


def _dot(a, b):
    return jnp.dot(a, b, preferred_element_type=F32)


def _dot_nt(a, b):
    return lax.dot_general(a, b, (((1,), (1,)), ((), ())), preferred_element_type=F32)


def _ada_kernel(c_ref, w_ref, b_ref, o_ref):
    c = c_ref[...]
    a = (c * jax.nn.sigmoid(c)).astype(BF16)
    o_ref[0] = _dot(a, w_ref[0].astype(BF16)) + b_ref[0]


def _ada_call(c_all, w_ada, b_ada):
    depth, d, n_out = w_ada.shape
    nb = c_all.shape[0]
    return pl.pallas_call(
        _ada_kernel,
        grid=(depth, n_out // ADA_TILE),
        in_specs=[
            pl.BlockSpec((nb, d), lambda l, j: (0, 0)),
            pl.BlockSpec((1, d, ADA_TILE), lambda l, j: (l, 0, j)),
            pl.BlockSpec((1, 1, ADA_TILE), lambda l, j: (l, 0, j)),
        ],
        out_specs=pl.BlockSpec((1, nb, ADA_TILE), lambda l, j: (l, 0, j)),
        out_shape=jax.ShapeDtypeStruct((depth, nb, n_out), F32),
        compiler_params=_cparams(2),
        name="ada_mod",
    )(c_all, w_ada, b_ada.reshape(depth, 1, n_out))


def _inproj_kernel(x_ref, sc_ref, sh_ref, g_ref, wm_ref, wf_ref, bf_ref, *refs,
                   seg, carry, prompt):
    if prompt:
        u_ref, k_ref, v_ref, lf_ref, qa_ref, ka_ref, vt_ref, carry_sc = refs
    else:
        u_ref, q_ref, k_ref, v_ref, lf_ref, fc_ref = refs
    x3 = x_ref[...]
    g_rows, r_rows, d = x3.shape
    n = g_rows * r_rows
    h = _norm_mod(x3, g_ref[0], sc_ref[...], sh_ref[...]).reshape(n, d).astype(BF16)
    z = _dot(h, wm_ref[0])
    w = z.shape[1] // 4
    u_ref[...] = z[:, :w].reshape(g_rows, r_rows, w)
    q = z[:, w:2 * w] * (HEAD_DIM ** -0.5 * (LOG2E if prompt else 1.0))
    k = z[:, 2 * w:3 * w]
    v = z[:, 3 * w:]
    if prompt:
        v_t = v.T
        k_ref[0] = k.T
        v_ref[0] = v_t
        vt_ref[0] = v_t.astype(BF16)
    else:
        k_ref[...] = k.reshape(g_rows, r_rows, w)
        v_ref[...] = v.reshape(g_rows, r_rows, w)
        q_ref[...] = q.reshape(g_rows, r_rows, w)

    fl = _dot(h, wf_ref[0]) + bf_ref[0]
    logf = jnp.minimum(fl, 0.0) - jnp.log1p(jnp.exp(-jnp.abs(fl)))
    n_heads = lf_ref.shape[1] if prompt else lf_ref.shape[-1]
    lane = lax.broadcasted_iota(jnp.int32, (1, LANES), 1)
    logf = jnp.where(lane < n_heads, logf, 0.0)
    if prompt:
        lf_ref[0] = logf.T[:n_heads, :]
    else:
        lf_ref[...] = logf[:, :n_heads].reshape(g_rows, r_rows, n_heads)

    row = lax.broadcasted_iota(jnp.int32, (n, n), 0)
    col = lax.broadcasted_iota(jnp.int32, (n, n), 1)
    keep = col <= row
    if seg < n:
        keep = keep & ((row // seg) == (col // seg))
    tri = jnp.where(keep, 1.0, 0.0).astype(BF16)
    hi, mid, lo = _split3(logf)
    f = _dot(tri, hi) + _dot(tri, mid) + _dot(tri, lo)
    if carry:
        @pl.when(pl.program_id(1) == 0)
        def _():
            carry_sc[...] = jnp.zeros_like(carry_sc)
        f = f + carry_sc[...]
        carry_sc[...] = f[n - 1:n, :]
    if not prompt:
        fc_ref[...] = f.reshape(g_rows, r_rows, LANES)
        return
    f2 = f * LOG2E
    for head in range(n_heads):
        pair = slice((head // 2) * LANES, (head // 2 + 1) * LANES)
        q_h, k_h = q[:, pair], k[:, pair]
        if head % 2:
            q_h = pltpu.roll(q_h, HEAD_DIM, axis=1)
            k_h = pltpu.roll(k_h, HEAD_DIM, axis=1)
        fh = jnp.broadcast_to(f2[:, head:head + 1], (n, LANES))
        hi = fh.astype(BF16).astype(F32)
        mid = (fh - hi).astype(BF16).astype(F32)
        lo = fh - hi - mid
        ones = jnp.ones_like(fh)
        zeros = jnp.zeros_like(fh)
        terms = lambda first, second: jnp.where(
            lane < HEAD_DIM + 3, first, jnp.where(lane < HEAD_DIM + 6, second, zeros))
        f_terms = jnp.where(lane % 3 == (HEAD_DIM + 0) % 3, hi, jnp.where(lane % 3 == (HEAD_DIM + 1) % 3, mid, lo))
        qa_ref[0, head] = jnp.where(lane < HEAD_DIM, q_h, terms(f_terms, ones)).astype(BF16)
        ka_ref[0, head] = jnp.where(lane < HEAD_DIM, k_h, terms(ones, -f_terms)).astype(BF16)


def _inproj_prompt(x, sc, sh, g, w_main, w_fl, b_fl, layer, n_heads):
    b, s, d = x.shape
    tm = PROMPT_TILE
    w = w_main.shape[2] // 4
    tile = lambda width: pl.BlockSpec((1, tm, width), lambda bi, i: (bi, i, 0))
    per_batch = pl.BlockSpec((1, 1, d), lambda bi, i: (bi, 0, 0))
    per_head = pl.BlockSpec((1, n_heads, tm, LANES), lambda bi, i: (bi, 0, i, 0))
    out_shape = [
        jax.ShapeDtypeStruct((b, s, w), F32),
        jax.ShapeDtypeStruct((b, w, s), F32),
        jax.ShapeDtypeStruct((b, w, s), F32),
        jax.ShapeDtypeStruct((b, n_heads, s), F32),
        jax.ShapeDtypeStruct((b, n_heads, s, LANES), BF16),
        jax.ShapeDtypeStruct((b, n_heads, s, LANES), BF16),
        jax.ShapeDtypeStruct((b, w, s), BF16),
    ]
    seq_minor = lambda rows: pl.BlockSpec((1, rows, tm), lambda bi, i: (bi, 0, i))
    out_specs = [tile(w), seq_minor(w), seq_minor(w), seq_minor(n_heads), per_head, per_head, seq_minor(w)]
    return pl.pallas_call(
        functools.partial(_inproj_kernel, seg=s, carry=True, prompt=True),
        grid=(b, s // tm),
        in_specs=[
            tile(d), per_batch, per_batch,
            _const_spec((1, 1, d), (layer, 0, 0)),
            _const_spec((1,) + w_main.shape[1:], (layer, 0, 0)),
            _const_spec((1,) + w_fl.shape[1:], (layer, 0, 0)),
            _const_spec((1, 1, LANES), (layer, 0, 0)),
        ],
        out_specs=out_specs,
        out_shape=out_shape,
        scratch_shapes=[pltpu.VMEM((1, LANES), F32)],
        compiler_params=_cparams(2),
        name="inproj_prompt",
    )(x, sc, sh, g, w_main, w_fl, b_fl)


def _inproj_sample(x, sc, sh, g, w_main, w_fl, b_fl, layer, n_heads):
    db, n_new, d = x.shape
    w = w_main.shape[2] // 4
    full = lambda width: pl.BlockSpec((db, n_new, width), lambda i: (0, 0, 0))
    per_seq = pl.BlockSpec((db, 1, d), lambda i: (0, 0, 0))
    out_shape = [
        jax.ShapeDtypeStruct((db, n_new, w), F32),
        jax.ShapeDtypeStruct((db, n_new, w), F32),
        jax.ShapeDtypeStruct((db, n_new, w), F32),
        jax.ShapeDtypeStruct((db, n_new, w), F32),
        jax.ShapeDtypeStruct((db, n_new, n_heads), F32),
        jax.ShapeDtypeStruct((db, n_new, LANES), F32),
    ]
    out_specs = [full(w), full(w), full(w), full(w), full(n_heads), full(LANES)]
    return pl.pallas_call(
        functools.partial(_inproj_kernel, seg=n_new, carry=False, prompt=False),
        grid=(1,),
        in_specs=[
            full(d), per_seq, per_seq,
            _const_spec((1, 1, d), (layer, 0, 0)),
            _const_spec((1,) + w_main.shape[1:], (layer, 0, 0)),
            _const_spec((1,) + w_fl.shape[1:], (layer, 0, 0)),
            _const_spec((1, 1, LANES), (layer, 0, 0)),
        ],
        out_specs=out_specs,
        out_shape=out_shape,
        compiler_params=_cparams(1),
        name="inproj_sample",
    )(x, sc, sh, g, w_main, w_fl, b_fl)


def _attn_prompt_kernel(qa_ref, ka_ref, vt_ref, o_ref):
    t = qa_ref.shape[2]
    i = pl.program_id(2)
    heads = range(qa_ref.shape[1])
    qs = [qa_ref[0, a] for a in heads]
    key = lax.broadcasted_iota(jnp.int32, (t, t), 0)
    qry = lax.broadcasted_iota(jnp.int32, (t, t), 1)

    def block(j, carry, diagonal):
        off = pl.multiple_of(j * t, t)
        scores = []
        for a in heads:
            s = _dot_nt(ka_ref[0, a, pl.ds(off, t), :], qs[a])
            if diagonal:
                s = jnp.where(key <= qry, s, -jnp.inf)
            scores.append(s)
        out = []
        for a in heads:
            m, l, acc = carry[a]
            vj = vt_ref[0, a * HEAD_DIM:(a + 1) * HEAD_DIM, pl.ds(off, t)]
            m_new = jnp.maximum(m, jnp.max(scores[a], axis=0, keepdims=True))
            alpha = jnp.exp2(m - m_new)
            p = jnp.exp2(scores[a] - m_new)
            l = alpha * l + jnp.sum(p, axis=0, keepdims=True)
            acc = alpha * acc + _dot(vj, p.astype(BF16))
            out.append((m_new, l, acc))
        return tuple(out)

    init = tuple((jnp.full((1, t), -jnp.inf, F32), jnp.zeros((1, t), F32), jnp.zeros((HEAD_DIM, t), F32))
                 for _ in heads)
    carry = lax.fori_loop(0, i, lambda j, c: block(j, c, False), init)
    carry = block(i, carry, True)
    o = jnp.concatenate([acc / l for _, l, acc in carry], axis=0)
    o_ref[0] = o.T.astype(o_ref.dtype)


def _attn_prompt(qa, ka, vt):
    b, n_heads, s, _ = qa.shape
    per_step = LANES // HEAD_DIM
    t = ATTN_TILE
    return pl.pallas_call(
        _attn_prompt_kernel,
        grid=(b, n_heads // per_step, s // t),
        in_specs=[
            pl.BlockSpec((1, per_step, t, LANES), lambda bi, hp, i: (bi, hp, i, 0)),
            pl.BlockSpec((1, per_step, s, LANES), lambda bi, hp, i: (bi, hp, 0, 0)),
            pl.BlockSpec((1, LANES, s), lambda bi, hp, i: (bi, hp, 0)),
        ],
        out_specs=pl.BlockSpec((1, t, LANES), lambda bi, hp, i: (bi, i, hp)),
        out_shape=jax.ShapeDtypeStruct((b, s, n_heads * HEAD_DIM), BF16),
        compiler_params=_cparams(3),
        name="attn_prompt",
    )(qa, ka, vt)


def _attn_sample_kernel(pt_ref, q_ref, kn_ref, vn_ref, fcn_ref, sel_ref, *refs, n_pages, n_heads):
    del pt_ref
    k_refs = refs[:n_pages]
    v_refs = refs[n_pages:2 * n_pages]
    lf_refs = refs[2 * n_pages:3 * n_pages]
    o_ref, s_sc, kpad_sc, vpad_sc = refs[3 * n_pages:]
    n_new, w = q_ref.shape[1], q_ref.shape[2]
    rows = n_heads * n_new

    q = q_ref[0]
    lane_head = lax.broadcasted_iota(jnp.int32, (n_heads, 1, w), 2) // HEAD_DIM
    head_idx = lax.broadcasted_iota(jnp.int32, (n_heads, 1, w), 0)
    own_lanes = lane_head == head_idx
    q_bd = jnp.where(own_lanes, q[None], 0.0).reshape(rows, w).astype(BF16)

    lf = jnp.concatenate([r[0, 0] for r in lf_refs], axis=0)
    n_lf = lf.shape[0]
    sums = _dot(jnp.concatenate(_split3(lf), axis=0), sel_ref[...])
    sums = sums[:n_lf] + sums[n_lf:2 * n_lf] + sums[2 * n_lf:]
    within, totals = sums[:, :PAGE_SIZE], sums[:, PAGE_SIZE:]

    fcn = fcn_ref[0]
    fq = jnp.stack([fcn[:, h:h + 1] for h in range(n_heads)], axis=0)
    fcn_t = jnp.concatenate([fcn, jnp.zeros((LANES - n_new, LANES), F32)], axis=0).T

    kpad_sc[...] = jnp.zeros_like(kpad_sc)
    vpad_sc[...] = jnp.zeros_like(vpad_sc)
    kpad_sc[0:n_new, :] = kn_ref[0].astype(BF16)
    vpad_sc[0:n_new, :] = vn_ref[0].astype(BF16)
    s_new = _dot_nt(q_bd, kpad_sc[...]).reshape(n_heads, n_new, PAGE_SIZE)
    s_new = s_new + (fq - fcn_t[:n_heads, :][:, None, :])
    tq = lax.broadcasted_iota(jnp.int32, (1, n_new, PAGE_SIZE), 1)
    tk = lax.broadcasted_iota(jnp.int32, (1, n_new, PAGE_SIZE), 2)
    s_sc[:, :, n_pages * PAGE_SIZE:] = jnp.where(tk <= tq, s_new, -jnp.inf)

    later = jnp.zeros((n_heads, PAGE_SIZE), F32)
    for p in reversed(range(n_pages)):
        s_p = _dot(q_bd, k_refs[p][0, 0].astype(BF16)).reshape(n_heads, n_new, PAGE_SIZE)
        bias = within[p * n_heads:(p + 1) * n_heads] + later
        later = later + totals[p * n_heads:(p + 1) * n_heads]
        s_sc[:, :, p * PAGE_SIZE:(p + 1) * PAGE_SIZE] = s_p + (fq + bias[:, None, :])

    s_all = s_sc[...]
    m = jnp.max(s_all, axis=2, keepdims=True)
    e = jnp.exp(s_all - m)
    denom = jnp.sum(e, axis=2, keepdims=True)
    e = e.reshape(rows, (n_pages + 1) * PAGE_SIZE).astype(BF16)
    acc = _dot(e[:, n_pages * PAGE_SIZE:], vpad_sc[...])
    for p in range(n_pages):
        acc = acc + _dot_nt(e[:, p * PAGE_SIZE:(p + 1) * PAGE_SIZE], v_refs[p][0, 0].astype(BF16))
    o3 = acc.reshape(n_heads, n_new, w) / denom
    o_ref[0] = jnp.sum(jnp.where(own_lanes, o3, 0.0), axis=0)


def _attn_sample(q, k_new, v_new, f_new, sel, cache_kt, cache_vt, cache_lft, page_table, layer):
    db, n_new, w = q.shape
    n_heads = cache_lft.shape[2]
    n_pages = page_table.shape[1]
    per_seq = lambda width: pl.BlockSpec((1, n_new, width), lambda b, pt: (b, 0, 0))

    def page_spec(shape, p):
        return pl.BlockSpec(shape, lambda b, pt: (layer, pt[b, p], 0, 0))

    in_specs = [per_seq(w), per_seq(w), per_seq(w), per_seq(LANES),
                pl.BlockSpec(sel.shape, lambda b, pt: (0, 0), pipeline_mode=pl.Buffered(1))]
    in_specs += [page_spec((1, 1, w, PAGE_SIZE), p) for p in range(n_pages)]
    in_specs += [page_spec((1, 1, w, PAGE_SIZE), p) for p in range(n_pages)]
    in_specs += [page_spec((1, 1, n_heads, PAGE_SIZE), p) for p in range(n_pages)]
    grid_spec = pltpu.PrefetchScalarGridSpec(
        num_scalar_prefetch=1,
        grid=(db,),
        in_specs=in_specs,
        out_specs=per_seq(w),
        scratch_shapes=[pltpu.VMEM((n_heads, n_new, (n_pages + 1) * PAGE_SIZE), F32),
                        pltpu.VMEM((PAGE_SIZE, w), BF16), pltpu.VMEM((PAGE_SIZE, w), BF16)],
    )
    return pl.pallas_call(
        functools.partial(_attn_sample_kernel, n_pages=n_pages, n_heads=n_heads),
        grid_spec=grid_spec,
        out_shape=jax.ShapeDtypeStruct((db, n_new, w), F32),
        compiler_params=_cparams(1),
        name="attn_sample",
    )(page_table, q, k_new, v_new, f_new, sel,
      *([cache_kt] * n_pages), *([cache_vt] * n_pages), *([cache_lft] * n_pages))


def _pool_branch(load, pos, wgrp_ref, ps_ref):
    outs = []
    for gi, win in enumerate(POOL_WINDOWS):
        lanes = slice(gi * POOL_GROUP_DIM, (gi + 1) * POOL_GROUP_DIM)
        cur = load(0, lanes)
        acc = cur
        for dist in range(1, win):
            acc = acc + load(dist, lanes)
        cnt = jnp.minimum(float(win), pos + 1.0)
        p = acc / cnt - cur
        outs.append(_dot(p.astype(BF16), wgrp_ref[0, gi]))
    return jnp.concatenate(outs, axis=1) * ps_ref[0]


def _mix_tail(x3, a_br, b_br, sc, sh, gt, g, wgate_ref, wbrp_ref, wbra_ref, wout_ref):
    g_rows, r_rows, d = x3.shape
    n = g_rows * r_rows
    h = _norm_mod(x3, g, sc, sh).reshape(n, d).astype(BF16)
    gates = _dot(h, wgate_ref[0])
    pool_out = _dot(a_br.astype(BF16), wbrp_ref[0])
    attn_out = _dot(b_br.astype(BF16), wbra_ref[0])
    merged = jax.nn.sigmoid(gates[:, :d]) * pool_out + jax.nn.sigmoid(gates[:, d:]) * attn_out
    y = _dot(merged.astype(BF16), wout_ref[0])
    return x3 + gt * y.reshape(g_rows, r_rows, d)


def _mix_prompt_kernel(x_ref, halo_ref, u_ref, bbr_ref, sc_ref, sh_ref, gt_ref, g_ref, wgate_ref, wgrp_ref,
                       ps_ref, wbrp_ref, wbra_ref, wout_ref, o_ref, ext_sc):
    i = pl.program_id(1)
    tm = x_ref.shape[1]

    @pl.when(i == 0)
    def _():
        ext_sc[0:POOL_HALO, :] = jnp.zeros((POOL_HALO, ext_sc.shape[1]), F32)

    @pl.when(i > 0)
    def _():
        ext_sc[0:POOL_HALO, :] = halo_ref[0]

    ext_sc[POOL_HALO:POOL_HALO + tm, :] = u_ref[0]
    pos = (i * tm + lax.broadcasted_iota(jnp.int32, (tm, 1), 0)).astype(F32)
    load = lambda dist, lanes: ext_sc[pl.ds(POOL_HALO - dist, tm), lanes]
    a_br = _pool_branch(load, pos, wgrp_ref, ps_ref)
    o_ref[...] = _mix_tail(x_ref[...], a_br, bbr_ref[0], sc_ref[...], sh_ref[...], gt_ref[...], g_ref[0],
                           wgate_ref, wbrp_ref, wbra_ref, wout_ref)


def _mix_sample_kernel(x_ref, ext_ref, bbr_ref, sc_ref, sh_ref, gt_ref, g_ref, wgate_ref, wgrp_ref,
                       ps_ref, wbrp_ref, wbra_ref, wout_ref, o_ref, *, past):
    db, n_new, _ = x_ref.shape
    n = db * n_new
    pos = (past + lax.broadcasted_iota(jnp.int32, (db, n_new, 1), 1)).astype(F32).reshape(n, 1)
    load = lambda dist, lanes: ext_ref[:, pl.ds(POOL_HALO - dist, n_new), lanes].reshape(n, POOL_GROUP_DIM)
    a_br = _pool_branch(load, pos, wgrp_ref, ps_ref)
    b_br = bbr_ref[...].reshape(n, bbr_ref.shape[2])
    o_ref[...] = _mix_tail(x_ref[...], a_br, b_br, sc_ref[...], sh_ref[...], gt_ref[...], g_ref[0],
                           wgate_ref, wbrp_ref, wbra_ref, wout_ref)


def _mix_weight_specs(layer, d, w_gate, w_grp, pool_scale3, w_brp, w_bra, w_out):
    return [
        _const_spec((1, 1, d), (layer, 0, 0)),
        _const_spec((1,) + w_gate.shape[1:], (layer, 0, 0)),
        _const_spec((1,) + w_grp.shape[1:], (layer, 0, 0, 0)),
        _const_spec((1,) + pool_scale3.shape[1:], (layer, 0, 0)),
        _const_spec((1,) + w_brp.shape[1:], (layer, 0, 0)),
        _const_spec((1,) + w_bra.shape[1:], (layer, 0, 0)),
        _const_spec((1,) + w_out.shape[1:], (layer, 0, 0)),
    ]


def _mix_prompt(x, u, b_br, sc, sh, gt, g, w_gate, w_grp, pool_scale3, w_brp, w_bra, w_out, layer):
    b, s, d = x.shape
    tm = PROMPT_TILE
    wp = u.shape[2]
    tile = lambda width: pl.BlockSpec((1, tm, width), lambda bi, i: (bi, i, 0))
    per_batch = pl.BlockSpec((1, 1, d), lambda bi, i: (bi, 0, 0))
    halo = pl.BlockSpec((1, POOL_HALO, wp),
                        lambda bi, i: (bi, jnp.maximum(i * (tm // POOL_HALO) - 1, 0), 0))
    return pl.pallas_call(
        _mix_prompt_kernel,
        grid=(b, s // tm),
        in_specs=[tile(d), halo, tile(wp), tile(b_br.shape[2]), per_batch, per_batch, per_batch]
        + _mix_weight_specs(layer, d, w_gate, w_grp, pool_scale3, w_brp, w_bra, w_out),
        out_specs=tile(d),
        out_shape=jax.ShapeDtypeStruct(x.shape, F32),
        scratch_shapes=[pltpu.VMEM((POOL_HALO + tm, wp), F32)],
        compiler_params=_cparams(2),
        name="mix_prompt",
    )(x, u, u, b_br, sc, sh, gt, g, w_gate, w_grp, pool_scale3, w_brp, w_bra, w_out)


def _mix_sample(x, ext, b_br, sc, sh, gt, g, w_gate, w_grp, pool_scale3, w_brp, w_bra, w_out, layer, past):
    db, n_new, d = x.shape
    full = lambda a: pl.BlockSpec(a.shape, lambda i: (0,) * a.ndim)
    return pl.pallas_call(
        functools.partial(_mix_sample_kernel, past=past),
        grid=(1,),
        in_specs=[full(x), full(ext), full(b_br), full(sc), full(sh), full(gt)]
        + _mix_weight_specs(layer, d, w_gate, w_grp, pool_scale3, w_brp, w_bra, w_out),
        out_specs=full(x),
        out_shape=jax.ShapeDtypeStruct(x.shape, F32),
        compiler_params=_cparams(1),
        name="mix_sample",
    )(x, ext, b_br, sc, sh, gt, g, w_gate, w_grp, pool_scale3, w_brp, w_bra, w_out)


def _ffn_kernel(x_ref, sc_ref, sh_ref, gt_ref, g_ref, wg_ref, wu_ref, wo_ref, gfin_ref, *out_refs,
                n_chunks, final):
    x3 = x_ref[...]
    g_rows, r_rows, d = x3.shape
    n = g_rows * r_rows
    h = _norm_mod(x3, g_ref[0], sc_ref[...], sh_ref[...]).reshape(n, d).astype(BF16)
    d_ff = wg_ref.shape[2]
    ck = d_ff // n_chunks
    y = jnp.zeros((n, d), F32)
    for c in range(n_chunks):
        gate = _dot(h, wg_ref[0, :, c * ck:(c + 1) * ck])
        up = _dot(h, wu_ref[0, :, c * ck:(c + 1) * ck])
        act = (gate * jax.nn.sigmoid(gate) * up).astype(BF16)
        y = y + _dot(act, wo_ref[0, c * ck:(c + 1) * ck, :])
    x_new = x3 + gt_ref[...] * y.reshape(g_rows, r_rows, d)
    out_refs[0][...] = x_new
    if final:
        ms = jnp.mean(x_new * x_new, axis=-1, keepdims=True)
        out_refs[1][...] = x_new * lax.rsqrt(ms + RMS_EPS) * gfin_ref[...]


def _ffn(x, sc, sh, gt, g, w_g, w_u, w_o, g_final, layer, final, rows_per_step, per_row_group_mod):
    n_groups, r_rows, d = x.shape
    if per_row_group_mod:
        grid = (1, 1)
        xspec = pl.BlockSpec(x.shape, lambda bi, i: (0, 0, 0))
        mspec = pl.BlockSpec(sc.shape, lambda bi, i: (0, 0, 0))
    else:
        grid = (n_groups, r_rows // rows_per_step)
        xspec = pl.BlockSpec((1, rows_per_step, d), lambda bi, i: (bi, i, 0))
        mspec = pl.BlockSpec((1, 1, d), lambda bi, i: (bi, 0, 0))
    n_out = 2 if final else 1
    outs = pl.pallas_call(
        functools.partial(_ffn_kernel, n_chunks=FFN_CHUNKS, final=final),
        grid=grid,
        in_specs=[xspec, mspec, mspec, mspec,
                  _const_spec((1, 1, d), (layer, 0, 0)),
                  _const_spec((1,) + w_g.shape[1:], (layer, 0, 0)),
                  _const_spec((1,) + w_u.shape[1:], (layer, 0, 0)),
                  _const_spec((1,) + w_o.shape[1:], (layer, 0, 0)),
                  _const_spec((1, d), (0, 0))],
        out_specs=[xspec] * n_out,
        out_shape=[jax.ShapeDtypeStruct(x.shape, F32)] * n_out,
        compiler_params=_cparams(2),
        name="ffn",
    )(x, sc, sh, gt, g, w_g, w_u, w_o, g_final)
    return outs


FFN_CHUNKS = 2


def _suffix_selector(n_heads):
    del n_heads
    idx = jnp.arange(PAGE_SIZE)
    later = idx[:, None] > idx[None, :]
    return jnp.concatenate([later, jnp.ones_like(later)], axis=1).astype(BF16)


def kernel(x_prompt, x_sample, cache_k, cache_v, cache_logf, state_pool, page_table, c_prompt, c_sample,
           w_ada, b_ada, g_mix, w_in, b_forget, w_pool_grp, pool_scale, w_br_pool, w_br_attn, w_out,
           g_ffn, w_ffn_in, w_ffn_out, g_final):
    b, s, d = x_prompt.shape
    db, n_new, _ = x_sample.shape
    depth = w_ada.shape[0]
    n_heads = b_forget.shape[1]
    att_w = n_heads * HEAD_DIM
    d_pool = pool_scale.shape[1]
    d_ff = w_ffn_out.shape[1]
    n_phys = cache_k.shape[1]
    past = page_table.shape[1] * PAGE_SIZE
    assert s % PROMPT_TILE == 0 and s % ATTN_TILE == 0 and n_new == 8 and d_pool == 4 * POOL_GROUP_DIM

    n_main = d_pool + 3 * att_w
    w_main = w_in[:, :, :n_main].astype(BF16)
    w_fl = jnp.pad(w_in[:, :, n_main:n_main + n_heads], ((0, 0), (0, 0), (0, LANES - n_heads))).astype(BF16)
    w_gate = w_in[:, :, n_main + n_heads:].astype(BF16)
    b_fl = jnp.pad(b_forget, ((0, 0), (0, LANES - n_heads))).reshape(depth, 1, LANES)
    w_grp = w_pool_grp.astype(BF16)
    w_brp = w_br_pool.astype(BF16)
    w_bra = w_br_attn.astype(BF16)
    w_o = w_out.astype(BF16)
    w_fg = w_ffn_in[:, :, :d_ff].astype(BF16)
    w_fu = w_ffn_in[:, :, d_ff:].astype(BF16)
    w_fo = w_ffn_out.astype(BF16)
    g_mix3 = g_mix.reshape(depth, 1, d)
    g_ffn3 = g_ffn.reshape(depth, 1, d)
    pool_scale3 = pool_scale.reshape(depth, 1, d_pool)
    g_fin2 = g_final.reshape(1, d)
    sel = _suffix_selector(n_heads)
    cache_kt = jnp.transpose(cache_k, (0, 1, 3, 4, 2)).reshape(depth, n_phys, att_w, PAGE_SIZE)
    cache_vt = jnp.transpose(cache_v, (0, 1, 3, 4, 2)).reshape(depth, n_phys, att_w, PAGE_SIZE)
    cache_lft = jnp.transpose(cache_logf, (0, 1, 3, 2))

    nb = b + db
    nb_pad = -(-nb // 8) * 8
    c_all = jnp.concatenate([c_prompt, c_sample, jnp.zeros((nb_pad - nb, d), F32)], axis=0)
    mods = _ada_call(c_all, w_ada, b_ada).reshape(depth, nb_pad, N_MOD, 1, d)

    xp, xs = x_prompt, x_sample
    kp, vp, lp, pp, ksm, vsm, lsm, psm = [], [], [], [], [], [], [], []
    yp = ys = None
    for l in range(depth):
        final = l == depth - 1
        sh1, sc1, gt1, sh2, sc2, gt2 = [mods[l, :b, m] for m in range(N_MOD)]
        u, k, v, logf, qa, ka, vt = _inproj_prompt(xp, sc1, sh1, g_mix3, w_main, w_fl, b_fl, l, n_heads)
        b_br = _attn_prompt(qa, ka, vt)
        x1 = _mix_prompt(xp, u, b_br, sc1, sh1, gt1, g_mix3, w_gate, w_grp, pool_scale3, w_brp, w_bra, w_o, l)
        outs = _ffn(x1, sc2, sh2, gt2, g_ffn3, w_fg, w_fu, w_fo, g_fin2, l, final, PROMPT_TILE, False)
        xp = outs[0]
        if final:
            yp = outs[1]
        kp.append(k)
        vp.append(v)
        lp.append(logf)
        pp.append(u[:, s - POOL_BUF:])

        sh1, sc1, gt1, sh2, sc2, gt2 = [mods[l, b:nb, m] for m in range(N_MOD)]
        u, q, k, v, logf, f_new = _inproj_sample(xs, sc1, sh1, g_mix3, w_main, w_fl, b_fl, l, n_heads)
        b_br = _attn_sample(q, k, v, f_new, sel, cache_kt, cache_vt, cache_lft, page_table, l)
        ext = jnp.concatenate([jnp.zeros((db, POOL_HALO - POOL_BUF, d_pool), F32), state_pool[l], u], axis=1)
        x1 = _mix_sample(xs, ext, b_br, sc1, sh1, gt1, g_mix3, w_gate, w_grp, pool_scale3, w_brp, w_bra, w_o,
                         l, past)
        outs = _ffn(x1, sc2, sh2, gt2, g_ffn3, w_fg, w_fu, w_fo, g_fin2, l, final, None, True)
        xs = outs[0]
        if final:
            ys = outs[1]
        ksm.append(k.reshape(db, n_new, n_heads, HEAD_DIM))
        vsm.append(v.reshape(db, n_new, n_heads, HEAD_DIM))
        lsm.append(logf)
        psm.append(ext[:, ext.shape[1] - POOL_BUF:])

    to_seq_major = lambda a: jnp.transpose(a.reshape(depth, b, n_heads, HEAD_DIM, s), (0, 1, 4, 2, 3))
    return (yp, ys, to_seq_major(jnp.stack(kp)), to_seq_major(jnp.stack(vp)),
            jnp.transpose(jnp.stack(lp), (0, 1, 3, 2)), jnp.stack(pp),
            jnp.stack(ksm), jnp.stack(vsm), jnp.stack(lsm), jnp.stack(psm))
```
